```python
import math
import jax, jax.numpy as jnp
from jax import lax
import numpy as np

D_MODEL = 1024
BATCH = 2
SEQ = 8192
DEPTH = 1
DEC_BATCH = 4
DEC_SEQ = 8192
PAST_LEN = 128

ATT_HEADS = 8
HEAD_DIM = 64
ATT_WIDTH = ATT_HEADS * HEAD_DIM
SSM_WIDTH = D_MODEL - ATT_WIDTH
SSM_GROUP_CH = 16
SSM_GROUPS = SSM_WIDTH // SSM_GROUP_CH
SSM_STATE = 64
MIX_WIDTH = ATT_WIDTH + SSM_WIDTH
IN_COLS = 3 * ATT_WIDTH + SSM_WIDTH
D_FF = 4 * D_MODEL
PLE_DIM = 256
NUM_BUCKETS = 32
REL_MAX_DISTANCE = 1024
DIL_WINDOWS = (128, 512, 2048)
DIL_RATES = (1, 4, 16)
DT_MIN = 0.001
DT_MAX = 0.1
RMS_EPS = 1e-6
NEG_INF = -1e30

kernel_name = 'hybrid_s5_dilated_attention_encoder'


def rms_norm(x, g):
    x32 = x.astype(jnp.float32)
    y = x32 * lax.rsqrt(jnp.mean(x32 * x32, axis=-1, keepdims=True) + RMS_EPS)
    return (y * g.astype(jnp.float32)).astype(x.dtype)


def t5_bucket(rel):
    half = NUM_BUCKETS // 2
    n = -rel
    ret = jnp.where(n < 0, half, 0)
    n = jnp.abs(n)
    max_exact = half // 2
    nf = jnp.maximum(n, 1).astype(jnp.float32)
    large = max_exact + (jnp.log(nf / max_exact) / math.log(REL_MAX_DISTANCE / max_exact)
                         * (half - max_exact)).astype(jnp.int32)
    large = jnp.minimum(large, half - 1)
    return ret + jnp.where(n < max_exact, n, large)


def dilated_branch(q, k, v, rel_bias, window, dilation):
    Bn, S, H, E = q.shape
    radius = window // (2 * dilation)
    blk = radius
    L = S // dilation
    nb = -(-L // blk)
    Lp = nb * blk

    def split(t):
        return t.reshape(Bn, L, dilation, H, E)

    qb = jnp.pad(split(q), ((0, 0), (0, Lp - L), (0, 0), (0, 0), (0, 0))).reshape(Bn, nb, blk, dilation, H, E)

    def windows(t):
        tp = jnp.pad(split(t), ((0, 0), (blk, Lp - L + blk), (0, 0), (0, 0), (0, 0)))
        tp = tp.reshape(Bn, nb + 2, blk, dilation, H, E)
        return jnp.concatenate([tp[:, :-2], tp[:, 1:-1], tp[:, 2:]], axis=2)

    kw = windows(k)
    vw = windows(v)
    offset = jnp.arange(3 * blk)[None, :] - blk - jnp.arange(blk)[:, None]
    key_m = (jnp.arange(nb)[:, None] - 1) * blk + jnp.arange(3 * blk)[None, :]
    mask = (jnp.abs(offset) <= radius)[None] & ((key_m >= 0) & (key_m < L))[:, None, :]
    bias = rel_bias[t5_bucket(offset * dilation)].astype(jnp.float32).transpose(2, 0, 1)
    s = jnp.einsum('bnqrhe,bnkrhe->bnrhqk', qb, kw) * (HEAD_DIM ** -0.5) + bias
    s = jnp.where(mask[None, :, None, None], s, NEG_INF)
    m = jnp.max(s, axis=-1, keepdims=True)
    pr = jnp.exp(s - m)
    den = jnp.sum(pr, axis=-1)
    o = jnp.einsum('bnrhqk,bnkrhe->bnqrhe', pr, vw) / den.transpose(0, 1, 4, 2, 3)[..., None]
    lse = (m[..., 0] + jnp.log(den)).transpose(0, 1, 4, 2, 3)
    o = o.reshape(Bn, Lp, dilation, H, E)[:, :L].reshape(Bn, S, H, E)
    lse = lse.reshape(Bn, Lp, dilation, H)[:, :L].reshape(Bn, S, H)
    return o, lse


def attention_mixer(q, k, v, rel_bias):
    Bn, S, _ = q.shape
    q = q.astype(jnp.float32).reshape(Bn, S, ATT_HEADS, HEAD_DIM)
    k = k.astype(jnp.float32).reshape(Bn, S, ATT_HEADS, HEAD_DIM)
    v = v.astype(jnp.float32).reshape(Bn, S, ATT_HEADS, HEAD_DIM)
    outs, lses = [], []
    for w, d in zip(DIL_WINDOWS, DIL_RATES):
        o, l = dilated_branch(q, k, v, rel_bias, w, d)
        outs.append(o)
        lses.append(l)
    wts = jax.nn.softmax(jnp.stack(lses, axis=0), axis=0)
    o = jnp.sum(wts[..., None] * jnp.stack(outs, axis=0), axis=0)
    return o.reshape(Bn, S, ATT_WIDTH)


def _complex_affine_combine(e1, e2):
    a1r, a1i, b1r, b1i = e1
    a2r, a2i, b2r, b2i = e2
    return (a2r * a1r - a2i * a1i,
            a2r * a1i + a2i * a1r,
            a2r * b1r - a2i * b1i + b2r,
            a2r * b1i + a2i * b1r + b2i)


def ssm_direction(u, a_re, a_im, log_dt, b_re, b_im, c_re, c_im, reverse):
    f32 = jnp.float32
    a_re = a_re.astype(f32)
    a_im = a_im.astype(f32)
    dt = jnp.exp(log_dt.astype(f32))[:, None]
    mag = jnp.exp(a_re * dt)
    ab_re = mag * jnp.cos(a_im * dt)
    ab_im = mag * jnp.sin(a_im * dt)
    inv = 1.0 / (a_re * a_re + a_im * a_im)
    f_re = ((ab_re - 1.0) * a_re + ab_im * a_im) * inv
    f_im = (ab_im * a_re - (ab_re - 1.0) * a_im) * inv
    b_re = b_re.astype(f32)
    b_im = b_im.astype(f32)
    bb_re = f_re[..., None] * b_re - f_im[..., None] * b_im
    bb_im = f_re[..., None] * b_im + f_im[..., None] * b_re
    bu_re = jnp.einsum('bsgh,gnh->bsgn', u, bb_re)
    bu_im = jnp.einsum('bsgh,gnh->bsgn', u, bb_im)
    elems = (jnp.broadcast_to(ab_re, bu_re.shape), jnp.broadcast_to(ab_im, bu_re.shape), bu_re, bu_im)
    _, _, h_re, h_im = lax.associative_scan(_complex_affine_combine, elems, reverse=reverse, axis=1)
    return (jnp.einsum('bsgn,ghn->bsgh', h_re, c_re.astype(f32))
            - jnp.einsum('bsgn,ghn->bsgh', h_im, c_im.astype(f32)))


def ssm_mixer(u, a_re, a_im, log_dt, b_re, b_im, c_re, c_im, d, w_glu, b_glu):
    Bn, S, _ = u.shape
    u32 = u.astype(jnp.float32).reshape(Bn, S, SSM_GROUPS, SSM_GROUP_CH)
    y = d.astype(jnp.float32) * u32
    for direction in range(2):
        y = y + ssm_direction(u32, a_re[direction], a_im[direction], log_dt[direction],
                              b_re[direction], b_im[direction], c_re[direction], c_im[direction],
                              reverse=(direction == 1))
    g = jax.nn.gelu(y.reshape(Bn, S, SSM_WIDTH))
    out = g * jax.nn.sigmoid(g @ w_glu.astype(jnp.float32) + b_glu.astype(jnp.float32))
    return out


def encoder_layer(h, p_i, rel_bias, g_mix, w_in, a_re, a_im, log_dt, b_re, b_im, c_re, c_im, d,
                  w_glu, b_glu, g_att_out, g_ssm_out, w_out, g_mlp, w_mlp1, w_mlp2,
                  g_ple, w_ple_gate, w_ple_proj):
    a = rms_norm(h, g_mix)
    z = a @ w_in
    q = z[..., :ATT_WIDTH]
    k = z[..., ATT_WIDTH:2 * ATT_WIDTH]
    v = z[..., 2 * ATT_WIDTH:3 * ATT_WIDTH]
    u = z[..., 3 * ATT_WIDTH:]
    att = attention_mixer(q, k, v, rel_bias).astype(h.dtype)
    ssm = ssm_mixer(u, a_re, a_im, log_dt, b_re, b_im, c_re, c_im, d, w_glu, b_glu).astype(h.dtype)
    mix = jnp.concatenate([rms_norm(att, g_att_out), rms_norm(ssm, g_ssm_out)], axis=-1)
    h = h + mix @ w_out
    f = rms_norm(h, g_mlp)
    h = h + jnp.square(jax.nn.relu(f @ w_mlp1)) @ w_mlp2
    e = rms_norm(h, g_ple)
    h = h + jax.nn.sigmoid(e @ w_ple_gate) * (p_i @ w_ple_proj)
    return h


def encoder_trunk(x, p, weights):
    (rel_bias, g_mix, w_in, ssm_a_re, ssm_a_im, ssm_log_dt, ssm_b_re, ssm_b_im, ssm_c_re, ssm_c_im,
     ssm_d, w_glu, b_glu, g_att_out, g_ssm_out, w_out, g_mlp, w_mlp1, w_mlp2, g_ple, w_ple_gate,
     w_ple_proj, g_final) = weights
    h = x
    for i in range(DEPTH):
        h = encoder_layer(h, p[i], rel_bias, g_mix[i], w_in[i], ssm_a_re[i], ssm_a_im[i], ssm_log_dt[i],
                          ssm_b_re[i], ssm_b_im[i], ssm_c_re[i], ssm_c_im[i], ssm_d[i], w_glu[i], b_glu[i],
                          g_att_out[i], g_ssm_out[i], w_out[i], g_mlp[i], w_mlp1[i], w_mlp2[i],
                          g_ple[i], w_ple_gate[i], w_ple_proj[i])
    return rms_norm(h, g_final)


def setup_inputs(seed: int = 0) -> dict:
    key = jax.random.key(seed)
    ks = jax.random.split(key, 32)
    f32 = jnp.float32

    def nrm(k, shape, s):
        return jax.random.normal(k, shape, f32) * s

    G, N, HC = SSM_GROUPS, SSM_STATE, SSM_GROUP_CH
    n_idx = jnp.arange(N, dtype=f32)
    return {
        'x_prompt': nrm(ks[0], (BATCH, SEQ, D_MODEL), 1.0),
        'x_sample': nrm(ks[1], (DEC_BATCH, DEC_SEQ, D_MODEL), 1.0),
        'p_prompt': nrm(ks[2], (DEPTH, BATCH, SEQ, PLE_DIM), 1.0),
        'p_sample': nrm(ks[3], (DEPTH, DEC_BATCH, DEC_SEQ, PLE_DIM), 1.0),
        'rel_bias': nrm(ks[4], (NUM_BUCKETS, ATT_HEADS), 0.5),
        'g_mix': 1.0 + nrm(ks[5], (DEPTH, D_MODEL), 0.02),
        'w_in': nrm(ks[6], (DEPTH, D_MODEL, IN_COLS), D_MODEL ** -0.5),
        'ssm_a_re': -0.5 + nrm(ks[7], (DEPTH, 2, G, N), 0.01),
        'ssm_a_im': math.pi * n_idx + nrm(ks[8], (DEPTH, 2, G, N), 0.01),
        'ssm_log_dt': jax.random.uniform(ks[9], (DEPTH, 2, G), f32, math.log(DT_MIN), math.log(DT_MAX)),
        'ssm_b_re': nrm(ks[10], (DEPTH, 2, G, N, HC), (2 * HC) ** -0.5),
        'ssm_b_im': nrm(ks[11], (DEPTH, 2, G, N, HC), (2 * HC) ** -0.5),
        'ssm_c_re': nrm(ks[12], (DEPTH, 2, G, HC, N), N ** -0.5),
        'ssm_c_im': nrm(ks[13], (DEPTH, 2, G, HC, N), N ** -0.5),
        'ssm_d': nrm(ks[14], (DEPTH, G, HC), 0.5),
        'w_glu': nrm(ks[15], (DEPTH, SSM_WIDTH, SSM_WIDTH), SSM_WIDTH ** -0.5),
        'b_glu': nrm(ks[16], (DEPTH, SSM_WIDTH), 0.02),
        'g_att_out': 1.0 + nrm(ks[17], (DEPTH, ATT_WIDTH), 0.02),
        'g_ssm_out': 1.0 + nrm(ks[18], (DEPTH, SSM_WIDTH), 0.02),
        'w_out': nrm(ks[19], (DEPTH, MIX_WIDTH, D_MODEL), MIX_WIDTH ** -0.5),
        'g_mlp': 1.0 + nrm(ks[20], (DEPTH, D_MODEL), 0.02),
        'w_mlp1': nrm(ks[21], (DEPTH, D_MODEL, D_FF), D_MODEL ** -0.5),
        'w_mlp2': nrm(ks[22], (DEPTH, D_FF, D_MODEL), D_FF ** -0.5),
        'g_ple': 1.0 + nrm(ks[23], (DEPTH, D_MODEL), 0.02),
        'w_ple_gate': nrm(ks[24], (DEPTH, D_MODEL, D_MODEL), D_MODEL ** -0.5),
        'w_ple_proj': nrm(ks[25], (DEPTH, PLE_DIM, D_MODEL), PLE_DIM ** -0.5),
        'g_final': 1.0 + nrm(ks[26], (D_MODEL,), 0.02),
    }


def reference(x_prompt, x_sample, p_prompt, p_sample, rel_bias, g_mix, w_in, ssm_a_re, ssm_a_im, ssm_log_dt,
              ssm_b_re, ssm_b_im, ssm_c_re, ssm_c_im, ssm_d, w_glu, b_glu, g_att_out, g_ssm_out, w_out,
              g_mlp, w_mlp1, w_mlp2, g_ple, w_ple_gate, w_ple_proj, g_final):
    weights = (rel_bias, g_mix, w_in, ssm_a_re, ssm_a_im, ssm_log_dt, ssm_b_re, ssm_b_im, ssm_c_re, ssm_c_im,
               ssm_d, w_glu, b_glu, g_att_out, g_ssm_out, w_out, g_mlp, w_mlp1, w_mlp2, g_ple, w_ple_gate,
               w_ple_proj, g_final)
    y_prompt = encoder_trunk(x_prompt, p_prompt, weights)
    y_sample = encoder_trunk(x_sample, p_sample, weights)
    return (y_prompt, y_sample)
```

```python
import functools
import math

import jax
import jax.numpy as jnp
import numpy as np
from jax import lax
from jax.experimental import pallas as pl
from jax.experimental.pallas import tpu as pltpu

F32 = jnp.float32
BF16 = jnp.bfloat16

D_MODEL = 1024
ATT_HEADS = 8
HEAD_DIM = 64
ATT_WIDTH = ATT_HEADS * HEAD_DIM
SSM_WIDTH = D_MODEL - ATT_WIDTH
SSM_GROUP_CH = 16
SSM_GROUPS = SSM_WIDTH // SSM_GROUP_CH
SSM_STATE = 64
IN_COLS = 3 * ATT_WIDTH + SSM_WIDTH
D_FF = 4 * D_MODEL
PLE_DIM = 256
NUM_BUCKETS = 32
REL_MAX_DISTANCE = 1024
DIL_WINDOWS = (128, 512, 2048)
DIL_RATES = (1, 4, 16)
RMS_EPS = 1e-6
NEG_INF = -1e30

LANES = 128
SUBLANES = 8
VMEM_LIMIT = 56 * 1024 * 1024

RADIUS = 64
Q_BLK = 128
K_BLK = Q_BLK + 2 * RADIUS
ATT_TILE = 2048
HEADS_PER_STEP = LANES // HEAD_DIM
N_VARIANTS = 3

CHUNK = 16
CHUNK_COLS = CHUNK * SSM_GROUP_CH
SEQ_PAD = SUBLANES
SSM_ROW_BLK = 512

IN_TILE = 512
POST_TILE = 512
FF_BLK = 1024


def _rms(x, g):
    return x * lax.rsqrt(jnp.mean(x * x, axis=-1, keepdims=True) + RMS_EPS) * g


def _sigmoid(x):
    return 1.0 / (1.0 + jnp.exp(-x))


def _gelu_tanh(x):
    c = math.sqrt(2.0 / math.pi)
    return 0.5 * x * (1.0 + jnp.tanh(c * (x + 0.044715 * (x * x * x))))


def _bdot(a, b):
    return jnp.dot(a.astype(BF16), b.astype(BF16), preferred_element_type=F32)


def _inproj_kernel(x_ref, g_ref, w_ref, qkv_ref, u_ref):
    a = _rms(x_ref[...], g_ref[...])
    z = _bdot(a, w_ref[...])
    qkv_ref[...] = z[:, :3 * ATT_WIDTH]
    u_ref[...] = z[:, 3 * ATT_WIDTH:]


def _inproj(x, g_mix, w_in_bf):
    t = x.shape[0]
    return pl.pallas_call(
        _inproj_kernel,
        grid=(t // IN_TILE,),
        in_specs=[
            pl.BlockSpec((IN_TILE, D_MODEL), lambda i: (i, 0)),
            pl.BlockSpec((1, D_MODEL), lambda i: (0, 0)),
            pl.BlockSpec((D_MODEL, IN_COLS), lambda i: (0, 0)),
        ],
        out_specs=[
            pl.BlockSpec((IN_TILE, 3 * ATT_WIDTH), lambda i: (i, 0)),
            pl.BlockSpec((IN_TILE, SSM_WIDTH), lambda i: (i, 0)),
        ],
        out_shape=[
            jax.ShapeDtypeStruct((t, 3 * ATT_WIDTH), F32),
            jax.ShapeDtypeStruct((t, SSM_WIDTH), F32),
        ],
        compiler_params=pltpu.CompilerParams(
            dimension_semantics=("arbitrary",), vmem_limit_bytes=VMEM_LIMIT),
        name="inproj",
    )(x, g_mix.reshape(1, D_MODEL), w_in_bf)


def _t5_bucket_np(rel):
    half = NUM_BUCKETS // 2
    n = -rel
    ret = np.where(n < 0, half, 0)
    n = np.abs(n)
    max_exact = half // 2
    nf = np.maximum(n, 1).astype(np.float64)
    large = max_exact + (np.log(nf / max_exact) / math.log(REL_MAX_DISTANCE / max_exact)
                         * (half - max_exact)).astype(np.int64)
    large = np.minimum(large, half - 1)
    return ret + np.where(n < max_exact, n, large)


def _bucket_tables():
    qi = np.arange(Q_BLK)[:, None]
    ci = np.arange(K_BLK)[None, :]
    out = np.zeros((len(DIL_RATES), N_VARIANTS, Q_BLK, K_BLK), np.int32)
    for b, d in enumerate(DIL_RATES):
        for v, shift in enumerate((0, -RADIUS, -2 * RADIUS)):
            off = ci - qi + shift
            bk = _t5_bucket_np(off * d)
            out[b, v] = np.where(np.abs(off) <= RADIUS, bk, NUM_BUCKETS)
    return out.reshape(len(DIL_RATES) * N_VARIANTS, Q_BLK, K_BLK)


def _bias_kernel(rel_ref, bk_ref, tab_ref):
    hp = pl.program_id(0)
    n_bv = bk_ref.shape[0]
    for bv in range(n_bv):
        bk = bk_ref[bv]
        for h2 in range(HEADS_PER_STEP):
            acc = jnp.full(bk.shape, NEG_INF, F32)
            for b in range(NUM_BUCKETS):
                acc = jnp.where(bk == b, rel_ref[b, hp * HEADS_PER_STEP + h2], acc)
            tab_ref[0, bv * HEADS_PER_STEP + h2] = acc


def _bias_tables(rel_bias):
    bk = jnp.asarray(_bucket_tables())
    n_bv = bk.shape[0]
    n_hp = ATT_HEADS // HEADS_PER_STEP
    return pl.pallas_call(
        _bias_kernel,
        grid=(n_hp,),
        in_specs=[
            pl.BlockSpec(memory_space=pltpu.SMEM),
            pl.BlockSpec((n_bv, Q_BLK, K_BLK), lambda h: (0, 0, 0)),
        ],
        out_specs=pl.BlockSpec((1, n_bv * HEADS_PER_STEP, Q_BLK, K_BLK), lambda h: (h, 0, 0, 0)),
        out_shape=jax.ShapeDtypeStruct((n_hp, n_bv * HEADS_PER_STEP, Q_BLK, K_BLK), F32),
        compiler_params=pltpu.CompilerParams(dimension_semantics=("arbitrary",)),
        name="bias_tables",
    )(rel_bias, bk)


def _att_kernel(q_ref, k_ref, v_ref, tab_ref, o_ref, ob_scr, lse_scr, *, seq_len):
    t = pl.program_id(2)
    lane = lax.broadcasted_iota(jnp.int32, (Q_BLK, LANES), 1)
    head0 = lane < HEAD_DIM

    for b, d in enumerate(DIL_RATES):
        n_m = seq_len // d
        blocks_per_class = ATT_TILE // d // Q_BLK
        m_base = t * (ATT_TILE // d)

        def sub_block(idx, carry, b=b, d=d, n_m=n_m, m_base=m_base):
            r = idx % d
            i = idx // d
            q_row = r + d * Q_BLK * i
            m0 = m_base + Q_BLK * i
            k_start = jnp.clip(m0 - RADIUS, 0, n_m - K_BLK)
            variant = jnp.where(m0 < RADIUS, 0, jnp.where(m0 > n_m - Q_BLK - RADIUS, 2, 1))
            k_row = r + d * k_start
            if d == 1:
                q = q_ref[pl.ds(q_row, Q_BLK), :]
                k = k_ref[pl.ds(k_row, K_BLK), :]
                v = v_ref[pl.ds(k_row, K_BLK), :]
            else:
                q = q_ref[pl.ds(q_row, Q_BLK, stride=d), :]
                k = k_ref[pl.ds(k_row, K_BLK, stride=d), :]
                v = v_ref[pl.ds(k_row, K_BLK, stride=d), :]
            q = q * (HEAD_DIM ** -0.5)
            kb = k.astype(BF16)
            vb = v.astype(BF16)
            outs, lses = [], []
            for h2 in range(HEADS_PER_STEP):
                sel = head0 if h2 == 0 else jnp.logical_not(head0)
                qh = jnp.where(sel, q, 0.0).astype(BF16)
                s = lax.dot_general(qh, kb, (((1,), (1,)), ((), ())), preferred_element_type=F32)
                s = s + tab_ref[0, (b * N_VARIANTS + variant) * HEADS_PER_STEP + h2]
                m = jnp.max(s, axis=-1, keepdims=True)
                p = jnp.exp(s - m)
                den = jnp.sum(p, axis=-1, keepdims=True)
                outs.append(jnp.dot(p.astype(BF16), vb, preferred_element_type=F32) / den)
                lses.append(m + jnp.log(den))
            o = jnp.where(head0, outs[0], outs[1])
            lse = jnp.where(head0, lses[0], lses[1])
            if d == 1:
                ob_scr[b, pl.ds(q_row, Q_BLK), :] = o
                lse_scr[b, pl.ds(q_row, Q_BLK), :] = lse
            else:
                ob_scr.at[b][pl.ds(q_row, Q_BLK, stride=d), :] = o
                lse_scr.at[b][pl.ds(q_row, Q_BLK, stride=d), :] = lse
            return carry

        lax.fori_loop(0, d * blocks_per_class, sub_block, 0)

    lse_all = lse_scr[...]
    m = jnp.max(lse_all, axis=0)
    num = jnp.zeros((ATT_TILE, LANES), F32)
    den = jnp.zeros((ATT_TILE, LANES), F32)
    for b in range(len(DIL_RATES)):
        w = jnp.exp(lse_all[b] - m)
        num = num + w * ob_scr[b]
        den = den + w
    o_ref[...] = num / den


def _attention(qkv, tab, n_seq, seq_len):
    n_hp = ATT_HEADS // HEADS_PER_STEP
    n_t = seq_len // ATT_TILE
    n_br = len(DIL_RATES)
    return pl.pallas_call(
        functools.partial(_att_kernel, seq_len=seq_len),
        grid=(n_seq, n_hp, n_t),
        in_specs=[
            pl.BlockSpec((ATT_TILE, LANES), lambda s, h, t: (s * n_t + t, h)),
            pl.BlockSpec((seq_len, LANES), lambda s, h, t: (s, n_hp + h)),
            pl.BlockSpec((seq_len, LANES), lambda s, h, t: (s, 2 * n_hp + h)),
            pl.BlockSpec((1,) + tab.shape[1:], lambda s, h, t: (h, 0, 0, 0)),
        ],
        out_specs=pl.BlockSpec((ATT_TILE, LANES), lambda s, h, t: (s * n_t + t, h)),
        out_shape=jax.ShapeDtypeStruct((n_seq * seq_len, ATT_WIDTH), F32),
        scratch_shapes=[
            pltpu.VMEM((n_br, ATT_TILE, LANES), F32),
            pltpu.VMEM((n_br, ATT_TILE, LANES), F32),
        ],
        compiler_params=pltpu.CompilerParams(
            dimension_semantics=("arbitrary", "arbitrary", "arbitrary"),
            vmem_limit_bytes=VMEM_LIMIT),
        name="dilated_attention",
    )(qkv, qkv, qkv, tab)


def _ssm_weights(a_re, a_im, log_dt, b_re, b_im, c_re, c_im, d_skip):
    hi = lax.Precision.HIGHEST
    G, N, HC = SSM_GROUPS, SSM_STATE, SSM_GROUP_CH
    dt = jnp.exp(log_dt)[..., None]
    mag = jnp.exp(a_re * dt)
    ab_re = mag * jnp.cos(a_im * dt)
    ab_im = mag * jnp.sin(a_im * dt)
    inv = 1.0 / (a_re * a_re + a_im * a_im)
    f_re = ((ab_re - 1.0) * a_re + ab_im * a_im) * inv
    f_im = (ab_im * a_re - (ab_re - 1.0) * a_im) * inv
    bb_re = f_re[..., None] * b_re - f_im[..., None] * b_im
    bb_im = f_re[..., None] * b_im + f_im[..., None] * b_re
    prs, pis = [jnp.ones_like(ab_re)], [jnp.zeros_like(ab_re)]
    for _ in range(CHUNK):
        pr, pi = prs[-1], pis[-1]
        prs.append(pr * ab_re - pi * ab_im)
        pis.append(pr * ab_im + pi * ab_re)
    pr = jnp.stack(prs)
    pi = jnp.stack(pis)
    abr = pr[..., None] * bb_re - pi[..., None] * bb_im
    abi = pr[..., None] * bb_im + pi[..., None] * bb_re
    kern = (jnp.einsum('dgcn,tdgnk->tdgck', c_re, abr, precision=hi)
            - jnp.einsum('dgcn,tdgnk->tdgck', c_im, abi, precision=hi))

    jj = np.arange(CHUNK)[:, None]
    ii = np.arange(CHUNK)[None, :]
    lag_f = np.clip(ii - jj, 0, CHUNK - 1)
    lag_b = np.clip(jj - ii, 0, CHUNK - 1)
    kf = jnp.where((ii >= jj)[:, :, None, None, None], kern[:CHUNK, 0][lag_f], 0.0)
    kb = jnp.where((jj >= ii)[:, :, None, None, None], kern[:CHUNK, 1][lag_b], 0.0)
    m_tot = (kf + kb).transpose(2, 0, 4, 1, 3).reshape(G, CHUNK_COLS, CHUNK_COLS)

    def pad_lanes(x):
        return jnp.pad(x, ((0, 0), (0, 0), (0, LANES - N)))

    def state_in(ab, direction, rev):
        sel = ab[:CHUNK, direction]
        if rev:
            sel = sel[::-1]
        return pad_lanes(sel.transpose(1, 0, 3, 2).reshape(G, CHUNK_COLS, N))

    w_cat = jnp.concatenate(
        [m_tot, state_in(abr, 0, True), state_in(abi, 0, True),
         state_in(abr, 1, False), state_in(abi, 1, False)], axis=-1).astype(BF16)

    def state_out(direction, pr_sel, pi_sel):
        cr = c_re[direction].transpose(0, 2, 1)[:, :, None, :]
        ci = c_im[direction].transpose(0, 2, 1)[:, :, None, :]
        pr_s = pr_sel.transpose(1, 2, 0)[..., None]
        pi_s = pi_sel.transpose(1, 2, 0)[..., None]
        from_re = (cr * pr_s - ci * pi_s).reshape(G, N, CHUNK_COLS)
        from_im = (-cr * pi_s - ci * pr_s).reshape(G, N, CHUNK_COLS)
        padr = ((0, 0), (0, LANES - N), (0, 0))
        return jnp.pad(from_re, padr), jnp.pad(from_im, padr)

    f_out = state_out(0, pr[1:CHUNK + 1, 0], pi[1:CHUNK + 1, 0])
    b_out = state_out(1, pr[CHUNK:0:-1, 1], pi[CHUNK:0:-1, 1])
    c_pow = jnp.concatenate([f_out[0], f_out[1], b_out[0], b_out[1]], axis=1).astype(BF16)

    a_chunk = jnp.stack([pr[CHUNK, 0], pi[CHUNK, 0], pr[CHUNK, 1], pi[CHUNK, 1]], axis=1)
    a_chunk = jnp.pad(a_chunk, ((0, 0), (0, 0), (0, LANES - N)))
    d_tile = jnp.tile(d_skip[:, None, :], (1, CHUNK, 1)).reshape(G, 1, CHUNK_COLS)
    return w_cat, c_pow, a_chunk, d_tile


def _ssm_kernel(x_ref, w_ref, cp_ref, a_ref, d_ref, y_ref, g_scr, h_scr, *, n_chunks):
    n_rows = n_chunks * SEQ_PAD
    w = w_ref[0]
    d_tile = d_ref[0]
    for rb in range(n_rows // SSM_ROW_BLK):
        rows = pl.ds(rb * SSM_ROW_BLK, SSM_ROW_BLK)
        x = x_ref[0, rows, :]
        r = _bdot(x, w)
        y_ref[0, rows, :] = r[:, :CHUNK_COLS] + x * d_tile
        g_scr[rows, :] = r[:, CHUNK_COLS:]

    a = a_ref[0]
    shape = (SEQ_PAD, LANES)
    pfr = jnp.broadcast_to(a[0:1], shape)
    pfi = jnp.broadcast_to(a[1:2], shape)
    pbr = jnp.broadcast_to(a[2:3], shape)
    pbi = jnp.broadcast_to(a[3:4], shape)

    def step(c, carry):
        hfr, hfi, hbr, hbi = carry
        rf = pl.ds(pl.multiple_of(c * SEQ_PAD, SEQ_PAD), SEQ_PAD)
        rb_ = pl.ds(pl.multiple_of((n_chunks - 1 - c) * SEQ_PAD, SEQ_PAD), SEQ_PAD)
        h_scr[rf, 0 * LANES:1 * LANES] = hfr
        h_scr[rf, 1 * LANES:2 * LANES] = hfi
        h_scr[rb_, 2 * LANES:3 * LANES] = hbr
        h_scr[rb_, 3 * LANES:4 * LANES] = hbi
        gfr = g_scr[rf, 0 * LANES:1 * LANES]
        gfi = g_scr[rf, 1 * LANES:2 * LANES]
        gbr = g_scr[rb_, 2 * LANES:3 * LANES]
        gbi = g_scr[rb_, 3 * LANES:4 * LANES]
        return (pfr * hfr - pfi * hfi + gfr, pfr * hfi + pfi * hfr + gfi,
                pbr * hbr - pbi * hbi + gbr, pbr * hbi + pbi * hbr + gbi)

    zero = jnp.zeros(shape, F32)
    lax.fori_loop(0, n_chunks, step, (zero, zero, zero, zero))

    cp = cp_ref[0]
    for rb in range(n_rows // SSM_ROW_BLK):
        rows = pl.ds(rb * SSM_ROW_BLK, SSM_ROW_BLK)
        y_ref[0, rows, :] = y_ref[0, rows, :] + _bdot(h_scr[rows, :], cp)


def _ssm(xg, w_cat, c_pow, a_chunk, d_tile, n_chunks):
    n_rows = n_chunks * SEQ_PAD
    G = SSM_GROUPS
    return pl.pallas_call(
        functools.partial(_ssm_kernel, n_chunks=n_chunks),
        grid=(G,),
        in_specs=[
            pl.BlockSpec((1, n_rows, CHUNK_COLS), lambda g: (g, 0, 0)),
            pl.BlockSpec((1,) + w_cat.shape[1:], lambda g: (g, 0, 0)),
            pl.BlockSpec((1,) + c_pow.shape[1:], lambda g: (g, 0, 0)),
            pl.BlockSpec((1,) + a_chunk.shape[1:], lambda g: (g, 0, 0)),
            pl.BlockSpec((1,) + d_tile.shape[1:], lambda g: (g, 0, 0)),
        ],
        out_specs=pl.BlockSpec((1, n_rows, CHUNK_COLS), lambda g: (g, 0, 0)),
        out_shape=jax.ShapeDtypeStruct((G, n_rows, CHUNK_COLS), F32),
        scratch_shapes=[
            pltpu.VMEM((n_rows, 4 * LANES), F32),
            pltpu.VMEM((n_rows, 4 * LANES), F32),
        ],
        compiler_params=pltpu.CompilerParams(
            dimension_semantics=("arbitrary",), vmem_limit_bytes=VMEM_LIMIT),
        name="ssm_chunked",
    )(xg, w_cat, c_pow, a_chunk, d_tile)


def _post_kernel(x_ref, att_ref, ys_ref, p_ref, wglu_ref, bglu_ref, gatt_ref, gssm_ref, wout_ref,
                 gmlp_ref, w1_ref, w2_ref, gple_ref, wgate_ref, wproj_ref, gfin_ref, o_ref):
    g = _gelu_tanh(ys_ref[...])
    ssm = g * _sigmoid(_bdot(g, wglu_ref[...]) + bglu_ref[...])
    att_n = _rms(att_ref[...], gatt_ref[...])
    ssm_n = _rms(ssm, gssm_ref[...])
    h = x_ref[...] + (_bdot(att_n, wout_ref[:ATT_WIDTH, :]) + _bdot(ssm_n, wout_ref[ATT_WIDTH:, :]))
    f = _rms(h, gmlp_ref[...]).astype(BF16)
    acc = jnp.zeros_like(h)
    for kb in range(D_FF // FF_BLK):
        cols = slice(kb * FF_BLK, (kb + 1) * FF_BLK)
        t = jnp.dot(f, w1_ref[:, cols], preferred_element_type=F32)
        t = jnp.square(jnp.maximum(t, 0.0))
        acc = acc + _bdot(t, w2_ref[cols, :])
    h = h + acc
    e = _rms(h, gple_ref[...])
    h = h + _sigmoid(_bdot(e, wgate_ref[...])) * _bdot(p_ref[...], wproj_ref[...])
    o_ref[...] = _rms(h, gfin_ref[...])


def _post(x, att, ys, p, wts):
    t = x.shape[0]
    tile = lambda width: pl.BlockSpec((POST_TILE, width), lambda i: (i, 0))

    def resident(arr):
        return pl.BlockSpec(arr.shape, lambda i: (0,) * arr.ndim, pipeline_mode=pl.Buffered(1))

    return pl.pallas_call(
        _post_kernel,
        grid=(t // POST_TILE,),
        in_specs=[tile(D_MODEL), tile(ATT_WIDTH), tile(SSM_WIDTH), tile(PLE_DIM)]
                 + [resident(w) for w in wts],
        out_specs=tile(D_MODEL),
        out_shape=jax.ShapeDtypeStruct((t, D_MODEL), F32),
        compiler_params=pltpu.CompilerParams(
            dimension_semantics=("arbitrary",), vmem_limit_bytes=VMEM_LIMIT),
        name="post_mixers",
    )(x, att, ys, p, *wts)


def kernel(x_prompt, x_sample, p_prompt, p_sample, rel_bias, g_mix, w_in, ssm_a_re, ssm_a_im, ssm_log_dt, ssm_b_re, ssm_b_im, ssm_c_re, ssm_c_im, ssm_d, w_glu, b_glu, g_att_out, g_ssm_out, w_out, g_mlp, w_mlp1, w_mlp2, g_ple, w_ple_gate, w_ple_proj, g_final):
    assert g_mix.shape[0] == 1, "single-layer trunk"
    seq_len = x_prompt.shape[1]
    assert x_sample.shape[1] == seq_len and seq_len % ATT_TILE == 0
    n_p, n_s = x_prompt.shape[0], x_sample.shape[0]
    n_seq = n_p + n_s
    assert n_seq <= SEQ_PAD
    n_tok = n_seq * seq_len
    n_chunks = seq_len // CHUNK

    x = jnp.concatenate([x_prompt, x_sample], axis=0).reshape(n_tok, D_MODEL)
    p = jnp.concatenate([p_prompt[0], p_sample[0]], axis=0).reshape(n_tok, PLE_DIM)

    qkv, u = _inproj(x, g_mix[0], w_in[0].astype(BF16))

    tab = _bias_tables(rel_bias)
    att = _attention(qkv, tab, n_seq, seq_len)

    w_cat, c_pow, a_chunk, d_tile = _ssm_weights(
        ssm_a_re[0], ssm_a_im[0], ssm_log_dt[0], ssm_b_re[0], ssm_b_im[0],
        ssm_c_re[0], ssm_c_im[0], ssm_d[0])
    xg = u.reshape(n_seq, n_chunks, CHUNK, SSM_GROUPS, SSM_GROUP_CH).transpose(3, 1, 0, 2, 4)
    xg = jnp.pad(xg, ((0, 0), (0, 0), (0, SEQ_PAD - n_seq), (0, 0), (0, 0)))
    xg = xg.reshape(SSM_GROUPS, n_chunks * SEQ_PAD, CHUNK_COLS)
    yg = _ssm(xg, w_cat, c_pow, a_chunk, d_tile, n_chunks)
    ys = yg.reshape(SSM_GROUPS, n_chunks, SEQ_PAD, CHUNK, SSM_GROUP_CH)[:, :, :n_seq]
    ys = ys.transpose(2, 1, 3, 0, 4).reshape(n_tok, SSM_WIDTH)

    row = lambda v, n: v.reshape(1, n)
    wts = (w_glu[0].astype(BF16), row(b_glu[0], SSM_WIDTH), row(g_att_out[0], ATT_WIDTH),
           row(g_ssm_out[0], SSM_WIDTH), w_out[0].astype(BF16), row(g_mlp[0], D_MODEL),
           w_mlp1[0].astype(BF16), w_mlp2[0].astype(BF16), row(g_ple[0], D_MODEL),
           w_ple_gate[0].astype(BF16), w_ple_proj[0].astype(BF16), row(g_final, D_MODEL))
    out = _post(x, att, ys, p, wts).reshape(n_seq, seq_len, D_MODEL)
    return out[:n_p], out[n_p:]
```

```python
import functools
import math

import jax
import jax.numpy as jnp
import numpy as np
from jax import lax
from jax.experimental import pallas as pl
from jax.experimental.pallas import tpu as pltpu

F32 = jnp.float32
BF16 = jnp.bfloat16

D_MODEL = 1024
ATT_HEADS = 8
HEAD_DIM = 64
ATT_WIDTH = ATT_HEADS * HEAD_DIM
SSM_WIDTH = D_MODEL - ATT_WIDTH
SSM_GROUP_CH = 16
SSM_GROUPS = SSM_WIDTH // SSM_GROUP_CH
SSM_STATE = 64
IN_COLS = 3 * ATT_WIDTH + SSM_WIDTH
D_FF = 4 * D_MODEL
PLE_DIM = 256
NUM_BUCKETS = 32
REL_MAX_DISTANCE = 1024
DIL_WINDOWS = (128, 512, 2048)
DIL_RATES = (1, 4, 16)
RMS_EPS = 1e-6
NEG_INF = -1e30

LANES = 128
SUBLANES = 8
VMEM_LIMIT = 56 * 1024 * 1024

RADIUS = 64
Q_BLK = 128
K_BLK = Q_BLK + 2 * RADIUS
ATT_TILE = 2048
HEADS_PER_STEP = LANES // HEAD_DIM
ATT_GROUP = 4
N_VARIANTS = 3

CHUNK = 16
CHUNK_COLS = CHUNK * SSM_GROUP_CH
SEQ_PAD = SUBLANES
SSM_ROW_BLK = 512

IN_TILE = 512
POST_TILE = 512
FF_BLK = 1024


def _rms(x, g):
    return x * lax.rsqrt(jnp.mean(x * x, axis=-1, keepdims=True) + RMS_EPS) * g


def _sigmoid(x):
    return 1.0 / (1.0 + jnp.exp(-x))


def _gelu_tanh(x):
    c = math.sqrt(2.0 / math.pi)
    return 0.5 * x * (1.0 + jnp.tanh(c * (x + 0.044715 * (x * x * x))))


def _bdot(a, b):
    return jnp.dot(a.astype(BF16), b.astype(BF16), preferred_element_type=F32)


def _inproj_kernel(x_ref, g_ref, w_ref, qkv_ref, u_ref):
    a = _rms(x_ref[...], g_ref[...])
    z = _bdot(a, w_ref[...])
    qkv_ref[...] = z[:, :3 * ATT_WIDTH]
    u_ref[...] = z[:, 3 * ATT_WIDTH:]


def _inproj(x, g_mix, w_in_bf):
    t = x.shape[0]
    return pl.pallas_call(
        _inproj_kernel,
        grid=(t // IN_TILE,),
        in_specs=[
            pl.BlockSpec((IN_TILE, D_MODEL), lambda i: (i, 0)),
            pl.BlockSpec((1, D_MODEL), lambda i: (0, 0)),
            pl.BlockSpec((D_MODEL, IN_COLS), lambda i: (0, 0)),
        ],
        out_specs=[
            pl.BlockSpec((IN_TILE, 3 * ATT_WIDTH), lambda i: (i, 0)),
            pl.BlockSpec((IN_TILE, SSM_WIDTH), lambda i: (i, 0)),
        ],
        out_shape=[
            jax.ShapeDtypeStruct((t, 3 * ATT_WIDTH), F32),
            jax.ShapeDtypeStruct((t, SSM_WIDTH), F32),
        ],
        compiler_params=pltpu.CompilerParams(
            dimension_semantics=("arbitrary",), vmem_limit_bytes=VMEM_LIMIT),
        name="inproj",
    )(x, g_mix.reshape(1, D_MODEL), w_in_bf)


def _t5_bucket_np(rel):
    half = NUM_BUCKETS // 2
    n = -rel
    ret = np.where(n < 0, half, 0)
    n = np.abs(n)
    max_exact = half // 2
    nf = np.maximum(n, 1).astype(np.float64)
    large = max_exact + (np.log(nf / max_exact) / math.log(REL_MAX_DISTANCE / max_exact)
                         * (half - max_exact)).astype(np.int64)
    large = np.minimum(large, half - 1)
    return ret + np.where(n < max_exact, n, large)


def _bucket_tables():
    qi = np.arange(Q_BLK)[:, None]
    ci = np.arange(K_BLK)[None, :]
    out = np.zeros((len(DIL_RATES), N_VARIANTS, Q_BLK, K_BLK), np.int32)
    for b, d in enumerate(DIL_RATES):
        for v, shift in enumerate((0, -RADIUS, -2 * RADIUS)):
            off = ci - qi + shift
            bk = _t5_bucket_np(off * d)
            out[b, v] = np.where(np.abs(off) <= RADIUS, bk, NUM_BUCKETS)
    return out.reshape(len(DIL_RATES) * N_VARIANTS, Q_BLK, K_BLK)


def _bias_kernel(rel_ref, bk_ref, tab_ref):
    hp = pl.program_id(0)
    n_bv = bk_ref.shape[0]
    for bv in range(n_bv):
        bk = bk_ref[bv]
        for h2 in range(HEADS_PER_STEP):
            acc = jnp.full(bk.shape, NEG_INF, F32)
            for b in range(NUM_BUCKETS):
                acc = jnp.where(bk == b, rel_ref[b, hp * HEADS_PER_STEP + h2], acc)
            tab_ref[0, bv * HEADS_PER_STEP + h2] = acc


def _bias_tables(rel_bias):
    bk = jnp.asarray(_bucket_tables())
    n_bv = bk.shape[0]
    n_hp = ATT_HEADS // HEADS_PER_STEP
    return pl.pallas_call(
        _bias_kernel,
        grid=(n_hp,),
        in_specs=[
            pl.BlockSpec(memory_space=pltpu.SMEM),
            pl.BlockSpec((n_bv, Q_BLK, K_BLK), lambda h: (0, 0, 0)),
        ],
        out_specs=pl.BlockSpec((1, n_bv * HEADS_PER_STEP, Q_BLK, K_BLK), lambda h: (h, 0, 0, 0)),
        out_shape=jax.ShapeDtypeStruct((n_hp, n_bv * HEADS_PER_STEP, Q_BLK, K_BLK), F32),
        compiler_params=pltpu.CompilerParams(dimension_semantics=("arbitrary",)),
        name="bias_tables",
    )(rel_bias, bk)


def _att_kernel(q_ref, k_ref, v_ref, tab_ref, o_ref, ob_scr, lse_scr, s_scr, p_scr, v_scr, *, seq_len):
    t = pl.program_id(2)
    lane = lax.broadcasted_iota(jnp.int32, (Q_BLK, LANES), 1)
    head0 = lane < HEAD_DIM
    n_sub = ATT_TILE // Q_BLK

    for b, d in enumerate(DIL_RATES):
        n_m = seq_len // d
        m_base = t * (ATT_TILE // d)

        def group(gi, carry, b=b, d=d, n_m=n_m, m_base=m_base):
            q_rows = []
            for j in range(ATT_GROUP):
                idx = gi * ATT_GROUP + j
                r = idx % d
                i = idx // d
                q_row = r + d * Q_BLK * i
                m0 = m_base + Q_BLK * i
                k_start = jnp.clip(m0 - RADIUS, 0, n_m - K_BLK)
                variant = jnp.where(m0 < RADIUS, 0, jnp.where(m0 > n_m - Q_BLK - RADIUS, 2, 1))
                k_row = r + d * k_start
                if d == 1:
                    q = q_ref[pl.ds(q_row, Q_BLK), :]
                    k = k_ref[pl.ds(k_row, K_BLK), :]
                    v = v_ref[pl.ds(k_row, K_BLK), :]
                else:
                    q = q_ref[pl.ds(q_row, Q_BLK, stride=d), :]
                    k = k_ref[pl.ds(k_row, K_BLK, stride=d), :]
                    v = v_ref[pl.ds(k_row, K_BLK, stride=d), :]
                q_rows.append(q_row)
                q = q * (HEAD_DIM ** -0.5)
                kb = k.astype(BF16)
                v_scr[j] = v.astype(BF16)
                for h2 in range(HEADS_PER_STEP):
                    sel = head0 if h2 == 0 else jnp.logical_not(head0)
                    qh = jnp.where(sel, q, 0.0).astype(BF16)
                    s = lax.dot_general(qh, kb, (((1,), (1,)), ((), ())), preferred_element_type=F32)
                    s_scr[j * HEADS_PER_STEP + h2] = (
                        s + tab_ref[0, (b * N_VARIANTS + variant) * HEADS_PER_STEP + h2])
            s_all = s_scr[...]
            m = jnp.max(s_all, axis=-1, keepdims=True)
            p = jnp.exp(s_all - m)
            den = jnp.sum(p, axis=-1, keepdims=True)
            p_scr[...] = p.astype(BF16)
            inv = 1.0 / den
            lse_all = m + jnp.log(den)
            for j in range(ATT_GROUP):
                outs = [jnp.dot(p_scr[j * HEADS_PER_STEP + h2], v_scr[j], preferred_element_type=F32)
                        * inv[j * HEADS_PER_STEP + h2] for h2 in range(HEADS_PER_STEP)]
                o = jnp.where(head0, outs[0], outs[1])
                lse = jnp.where(head0, lse_all[j * HEADS_PER_STEP], lse_all[j * HEADS_PER_STEP + 1])
                if d == 1:
                    ob_scr[b, pl.ds(q_rows[j], Q_BLK), :] = o
                    lse_scr[b, pl.ds(q_rows[j], Q_BLK), :] = lse
                else:
                    ob_scr.at[b][pl.ds(q_rows[j], Q_BLK, stride=d), :] = o
                    lse_scr.at[b][pl.ds(q_rows[j], Q_BLK, stride=d), :] = lse
            return carry

        lax.fori_loop(0, n_sub // ATT_GROUP, group, 0)

    lse_all = lse_scr[...]
    m = jnp.max(lse_all, axis=0)
    num = jnp.zeros((ATT_TILE, LANES), F32)
    den = jnp.zeros((ATT_TILE, LANES), F32)
    for b in range(len(DIL_RATES)):
        w = jnp.exp(lse_all[b] - m)
        num = num + w * ob_scr[b]
        den = den + w
    o_ref[...] = num / den


def _attention(qkv, tab, n_seq, seq_len):
    n_hp = ATT_HEADS // HEADS_PER_STEP
    n_t = seq_len // ATT_TILE
    n_br = len(DIL_RATES)
    return pl.pallas_call(
        functools.partial(_att_kernel, seq_len=seq_len),
        grid=(n_seq, n_hp, n_t),
        in_specs=[
            pl.BlockSpec((ATT_TILE, LANES), lambda s, h, t: (s * n_t + t, h)),
            pl.BlockSpec((seq_len, LANES), lambda s, h, t: (s, n_hp + h)),
            pl.BlockSpec((seq_len, LANES), lambda s, h, t: (s, 2 * n_hp + h)),
            pl.BlockSpec((1,) + tab.shape[1:], lambda s, h, t: (h, 0, 0, 0)),
        ],
        out_specs=pl.BlockSpec((ATT_TILE, LANES), lambda s, h, t: (s * n_t + t, h)),
        out_shape=jax.ShapeDtypeStruct((n_seq * seq_len, ATT_WIDTH), F32),
        scratch_shapes=[
            pltpu.VMEM((n_br, ATT_TILE, LANES), F32),
            pltpu.VMEM((n_br, ATT_TILE, LANES), F32),
            pltpu.VMEM((ATT_GROUP * HEADS_PER_STEP, Q_BLK, K_BLK), F32),
            pltpu.VMEM((ATT_GROUP * HEADS_PER_STEP, Q_BLK, K_BLK), BF16),
            pltpu.VMEM((ATT_GROUP, K_BLK, LANES), BF16),
        ],
        compiler_params=pltpu.CompilerParams(
            dimension_semantics=("arbitrary", "arbitrary", "arbitrary"),
            vmem_limit_bytes=VMEM_LIMIT),
        name="dilated_attention",
    )(qkv, qkv, qkv, tab)


def _ssm_weights(a_re, a_im, log_dt, b_re, b_im, c_re, c_im, d_skip):
    hi = lax.Precision.HIGHEST
    G, N, HC = SSM_GROUPS, SSM_STATE, SSM_GROUP_CH
    dt = jnp.exp(log_dt)[..., None]
    mag = jnp.exp(a_re * dt)
    ab_re = mag * jnp.cos(a_im * dt)
    ab_im = mag * jnp.sin(a_im * dt)
    inv = 1.0 / (a_re * a_re + a_im * a_im)
    f_re = ((ab_re - 1.0) * a_re + ab_im * a_im) * inv
    f_im = (ab_im * a_re - (ab_re - 1.0) * a_im) * inv
    bb_re = f_re[..., None] * b_re - f_im[..., None] * b_im
    bb_im = f_re[..., None] * b_im + f_im[..., None] * b_re
    prs, pis = [jnp.ones_like(ab_re)], [jnp.zeros_like(ab_re)]
    for _ in range(CHUNK):
        pr, pi = prs[-1], pis[-1]
        prs.append(pr * ab_re - pi * ab_im)
        pis.append(pr * ab_im + pi * ab_re)
    pr = jnp.stack(prs)
    pi = jnp.stack(pis)
    abr = pr[..., None] * bb_re - pi[..., None] * bb_im
    abi = pr[..., None] * bb_im + pi[..., None] * bb_re
    kern = (jnp.einsum('dgcn,tdgnk->tdgck', c_re, abr, precision=hi)
            - jnp.einsum('dgcn,tdgnk->tdgck', c_im, abi, precision=hi))

    jj = np.arange(CHUNK)[:, None]
    ii = np.arange(CHUNK)[None, :]
    lag_f = np.clip(ii - jj, 0, CHUNK - 1)
    lag_b = np.clip(jj - ii, 0, CHUNK - 1)
    kf = jnp.where((ii >= jj)[:, :, None, None, None], kern[:CHUNK, 0][lag_f], 0.0)
    kb = jnp.where((jj >= ii)[:, :, None, None, None], kern[:CHUNK, 1][lag_b], 0.0)
    m_tot = (kf + kb).transpose(2, 0, 4, 1, 3).reshape(G, CHUNK_COLS, CHUNK_COLS)

    def pad_lanes(x):
        return jnp.pad(x, ((0, 0), (0, 0), (0, LANES - N)))

    def state_in(ab, direction, rev):
        sel = ab[:CHUNK, direction]
        if rev:
            sel = sel[::-1]
        return pad_lanes(sel.transpose(1, 0, 3, 2).reshape(G, CHUNK_COLS, N))

    w_cat = jnp.concatenate(
        [m_tot, state_in(abr, 0, True), state_in(abi, 0, True),
         state_in(abr, 1, False), state_in(abi, 1, False)], axis=-1).astype(BF16)

    def state_out(direction, pr_sel, pi_sel):
        cr = c_re[direction].transpose(0, 2, 1)[:, :, None, :]
        ci = c_im[direction].transpose(0, 2, 1)[:, :, None, :]
        pr_s = pr_sel.transpose(1, 2, 0)[..., None]
        pi_s = pi_sel.transpose(1, 2, 0)[..., None]
        from_re = (cr * pr_s - ci * pi_s).reshape(G, N, CHUNK_COLS)
        from_im = (-cr * pi_s - ci * pr_s).reshape(G, N, CHUNK_COLS)
        padr = ((0, 0), (0, LANES - N), (0, 0))
        return jnp.pad(from_re, padr), jnp.pad(from_im, padr)

    f_out = state_out(0, pr[1:CHUNK + 1, 0], pi[1:CHUNK + 1, 0])
    b_out = state_out(1, pr[CHUNK:0:-1, 1], pi[CHUNK:0:-1, 1])
    c_pow = jnp.concatenate([f_out[0], f_out[1], b_out[0], b_out[1]], axis=1).astype(BF16)

    a_chunk = jnp.stack([pr[CHUNK, 0], pi[CHUNK, 0], pr[CHUNK, 1], pi[CHUNK, 1]], axis=1)
    a_chunk = jnp.pad(a_chunk, ((0, 0), (0, 0), (0, LANES - N)))
    d_tile = jnp.tile(d_skip[:, None, :], (1, CHUNK, 1)).reshape(G, 1, CHUNK_COLS)
    return w_cat, c_pow, a_chunk, d_tile


def _ssm_kernel(x_ref, w_ref, cp_ref, a_ref, d_ref, y_ref, g_scr, h_scr, *, n_chunks):
    n_rows = n_chunks * SEQ_PAD
    w = w_ref[0]
    d_tile = d_ref[0]
    for rb in range(n_rows // SSM_ROW_BLK):
        rows = pl.ds(rb * SSM_ROW_BLK, SSM_ROW_BLK)
        x = x_ref[0, rows, :]
        r = _bdot(x, w)
        y_ref[0, rows, :] = r[:, :CHUNK_COLS] + x * d_tile
        g_scr[rows, :] = r[:, CHUNK_COLS:]

    a = a_ref[0]
    shape = (SEQ_PAD, LANES)
    pfr = jnp.broadcast_to(a[0:1], shape)
    pfi = jnp.broadcast_to(a[1:2], shape)
    pbr = jnp.broadcast_to(a[2:3], shape)
    pbi = jnp.broadcast_to(a[3:4], shape)

    def step(c, carry):
        hfr, hfi, hbr, hbi = carry
        rf = pl.ds(pl.multiple_of(c * SEQ_PAD, SEQ_PAD), SEQ_PAD)
        rb_ = pl.ds(pl.multiple_of((n_chunks - 1 - c) * SEQ_PAD, SEQ_PAD), SEQ_PAD)
        h_scr[rf, 0 * LANES:1 * LANES] = hfr
        h_scr[rf, 1 * LANES:2 * LANES] = hfi
        h_scr[rb_, 2 * LANES:3 * LANES] = hbr
        h_scr[rb_, 3 * LANES:4 * LANES] = hbi
        gfr = g_scr[rf, 0 * LANES:1 * LANES]
        gfi = g_scr[rf, 1 * LANES:2 * LANES]
        gbr = g_scr[rb_, 2 * LANES:3 * LANES]
        gbi = g_scr[rb_, 3 * LANES:4 * LANES]
        return (pfr * hfr - pfi * hfi + gfr, pfr * hfi + pfi * hfr + gfi,
                pbr * hbr - pbi * hbi + gbr, pbr * hbi + pbi * hbr + gbi)

    zero = jnp.zeros(shape, F32)
    lax.fori_loop(0, n_chunks, step, (zero, zero, zero, zero))

    cp = cp_ref[0]
    for rb in range(n_rows // SSM_ROW_BLK):
        rows = pl.ds(rb * SSM_ROW_BLK, SSM_ROW_BLK)
        y_ref[0, rows, :] = y_ref[0, rows, :] + _bdot(h_scr[rows, :], cp)


def _ssm(xg, w_cat, c_pow, a_chunk, d_tile, n_chunks):
    n_rows = n_chunks * SEQ_PAD
    G = SSM_GROUPS
    return pl.pallas_call(
        functools.partial(_ssm_kernel, n_chunks=n_chunks),
        grid=(G,),
        in_specs=[
            pl.BlockSpec((1, n_rows, CHUNK_COLS), lambda g: (g, 0, 0)),
            pl.BlockSpec((1,) + w_cat.shape[1:], lambda g: (g, 0, 0)),
            pl.BlockSpec((1,) + c_pow.shape[1:], lambda g: (g, 0, 0)),
            pl.BlockSpec((1,) + a_chunk.shape[1:], lambda g: (g, 0, 0)),
            pl.BlockSpec((1,) + d_tile.shape[1:], lambda g: (g, 0, 0)),
        ],
        out_specs=pl.BlockSpec((1, n_rows, CHUNK_COLS), lambda g: (g, 0, 0)),
        out_shape=jax.ShapeDtypeStruct((G, n_rows, CHUNK_COLS), F32),
        scratch_shapes=[
            pltpu.VMEM((n_rows, 4 * LANES), F32),
            pltpu.VMEM((n_rows, 4 * LANES), F32),
        ],
        compiler_params=pltpu.CompilerParams(
            dimension_semantics=("arbitrary",), vmem_limit_bytes=VMEM_LIMIT),
        name="ssm_chunked",
    )(xg, w_cat, c_pow, a_chunk, d_tile)


def _post_kernel(x_ref, att_ref, ys_ref, p_ref, wglu_ref, bglu_ref, gatt_ref, gssm_ref, wout_ref,
                 gmlp_ref, w1_ref, w2_ref, gple_ref, wgate_ref, wproj_ref, gfin_ref, o_ref):
    g = _gelu_tanh(ys_ref[...])
    ssm = g * _sigmoid(_bdot(g, wglu_ref[...]) + bglu_ref[...])
    att_n = _rms(att_ref[...], gatt_ref[...])
    ssm_n = _rms(ssm, gssm_ref[...])
    h = x_ref[...] + (_bdot(att_n, wout_ref[:ATT_WIDTH, :]) + _bdot(ssm_n, wout_ref[ATT_WIDTH:, :]))
    f = _rms(h, gmlp_ref[...]).astype(BF16)
    acc = jnp.zeros_like(h)
    for kb in range(D_FF // FF_BLK):
        cols = slice(kb * FF_BLK, (kb + 1) * FF_BLK)
        t = jnp.dot(f, w1_ref[:, cols], preferred_element_type=F32)
        t = jnp.square(jnp.maximum(t, 0.0))
        acc = acc + _bdot(t, w2_ref[cols, :])
    h = h + acc
    e = _rms(h, gple_ref[...])
    h = h + _sigmoid(_bdot(e, wgate_ref[...])) * _bdot(p_ref[...], wproj_ref[...])
    o_ref[...] = _rms(h, gfin_ref[...])


def _post(x, att, ys, p, wts):
    t = x.shape[0]
    tile = lambda width: pl.BlockSpec((POST_TILE, width), lambda i: (i, 0))

    def resident(arr):
        return pl.BlockSpec(arr.shape, lambda i: (0,) * arr.ndim, pipeline_mode=pl.Buffered(1))

    return pl.pallas_call(
        _post_kernel,
        grid=(t // POST_TILE,),
        in_specs=[tile(D_MODEL), tile(ATT_WIDTH), tile(SSM_WIDTH), tile(PLE_DIM)]
                 + [resident(w) for w in wts],
        out_specs=tile(D_MODEL),
        out_shape=jax.ShapeDtypeStruct((t, D_MODEL), F32),
        compiler_params=pltpu.CompilerParams(
            dimension_semantics=("arbitrary",), vmem_limit_bytes=VMEM_LIMIT),
        name="post_mixers",
    )(x, att, ys, p, *wts)


def kernel(x_prompt, x_sample, p_prompt, p_sample, rel_bias, g_mix, w_in, ssm_a_re, ssm_a_im, ssm_log_dt, ssm_b_re, ssm_b_im, ssm_c_re, ssm_c_im, ssm_d, w_glu, b_glu, g_att_out, g_ssm_out, w_out, g_mlp, w_mlp1, w_mlp2, g_ple, w_ple_gate, w_ple_proj, g_final):
    assert g_mix.shape[0] == 1, "single-layer trunk"
    seq_len = x_prompt.shape[1]
    assert x_sample.shape[1] == seq_len and seq_len % ATT_TILE == 0
    n_p, n_s = x_prompt.shape[0], x_sample.shape[0]
    n_seq = n_p + n_s
    assert n_seq <= SEQ_PAD
    n_tok = n_seq * seq_len
    n_chunks = seq_len // CHUNK

    x = jnp.concatenate([x_prompt, x_sample], axis=0).reshape(n_tok, D_MODEL)
    p = jnp.concatenate([p_prompt[0], p_sample[0]], axis=0).reshape(n_tok, PLE_DIM)

    qkv, u = _inproj(x, g_mix[0], w_in[0].astype(BF16))

    tab = _bias_tables(rel_bias)
    att = _attention(qkv, tab, n_seq, seq_len)

    w_cat, c_pow, a_chunk, d_tile = _ssm_weights(
        ssm_a_re[0], ssm_a_im[0], ssm_log_dt[0], ssm_b_re[0], ssm_b_im[0],
        ssm_c_re[0], ssm_c_im[0], ssm_d[0])
    xg = u.reshape(n_seq, n_chunks, CHUNK, SSM_GROUPS, SSM_GROUP_CH).transpose(3, 1, 0, 2, 4)
    xg = jnp.pad(xg, ((0, 0), (0, 0), (0, SEQ_PAD - n_seq), (0, 0), (0, 0)))
    xg = xg.reshape(SSM_GROUPS, n_chunks * SEQ_PAD, CHUNK_COLS)
    yg = _ssm(xg, w_cat, c_pow, a_chunk, d_tile, n_chunks)
    ys = yg.reshape(SSM_GROUPS, n_chunks, SEQ_PAD, CHUNK, SSM_GROUP_CH)[:, :, :n_seq]
    ys = ys.transpose(2, 1, 3, 0, 4).reshape(n_tok, SSM_WIDTH)

    row = lambda v, n: v.reshape(1, n)
    wts = (w_glu[0].astype(BF16), row(b_glu[0], SSM_WIDTH), row(g_att_out[0], ATT_WIDTH),
           row(g_ssm_out[0], SSM_WIDTH), w_out[0].astype(BF16), row(g_mlp[0], D_MODEL),
           w_mlp1[0].astype(BF16), w_mlp2[0].astype(BF16), row(g_ple[0], D_MODEL),
           w_ple_gate[0].astype(BF16), w_ple_proj[0].astype(BF16), row(g_final, D_MODEL))
    out = _post(x, att, ys, p, wts).reshape(n_seq, seq_len, D_MODEL)
    return out[:n_p], out[n_p:]
```

```python
import functools
import math

import jax
import jax.numpy as jnp
import numpy as np
from jax import lax
from jax.experimental import pallas as pl
from jax.experimental.pallas import tpu as pltpu

F32 = jnp.float32
BF16 = jnp.bfloat16

D_MODEL = 1024
ATT_HEADS = 8
HEAD_DIM = 64
ATT_WIDTH = ATT_HEADS * HEAD_DIM
SSM_WIDTH = D_MODEL - ATT_WIDTH
SSM_GROUP_CH = 16
SSM_GROUPS = SSM_WIDTH // SSM_GROUP_CH
SSM_STATE = 64
IN_COLS = 3 * ATT_WIDTH + SSM_WIDTH
D_FF = 4 * D_MODEL
PLE_DIM = 256
NUM_BUCKETS = 32
REL_MAX_DISTANCE = 1024
DIL_WINDOWS = (128, 512, 2048)
DIL_RATES = (1, 4, 16)
RMS_EPS = 1e-6
NEG_INF = -1e30

LANES = 128
SUBLANES = 8
VMEM_LIMIT = 56 * 1024 * 1024

RADIUS = 64
Q_BLK = 128
K_BLK = Q_BLK + 2 * RADIUS
ATT_TILE = 2048
HEADS_PER_STEP = LANES // HEAD_DIM
ATT_GROUP = 4
N_VARIANTS = 3

CHUNK = 16
CHUNK_COLS = CHUNK * SSM_GROUP_CH
SEQ_PAD = SUBLANES

IN_TILE = 512
POST_TILE = 512
FF_BLK = 1024


def _rms(x, g):
    return x * lax.rsqrt(jnp.mean(x * x, axis=-1, keepdims=True) + RMS_EPS) * g


def _sigmoid(x):
    return 1.0 / (1.0 + jnp.exp(-x))


def _gelu_tanh(x):
    c = math.sqrt(2.0 / math.pi)
    return 0.5 * x * (1.0 + jnp.tanh(c * (x + 0.044715 * (x * x * x))))


def _bdot(a, b):
    return jnp.dot(a.astype(BF16), b.astype(BF16), preferred_element_type=F32)


def _inproj_kernel(x_ref, g_ref, w_ref, qkv_ref, u_ref):
    a = _rms(x_ref[...], g_ref[...])
    z = _bdot(a, w_ref[...])
    qkv_ref[...] = z[:, :3 * ATT_WIDTH]
    u_ref[...] = z[:, 3 * ATT_WIDTH:]


def _inproj(x, g_mix, w_in_bf):
    t = x.shape[0]
    return pl.pallas_call(
        _inproj_kernel,
        grid=(t // IN_TILE,),
        in_specs=[
            pl.BlockSpec((IN_TILE, D_MODEL), lambda i: (i, 0)),
            pl.BlockSpec((1, D_MODEL), lambda i: (0, 0)),
            pl.BlockSpec((D_MODEL, IN_COLS), lambda i: (0, 0)),
        ],
        out_specs=[
            pl.BlockSpec((IN_TILE, 3 * ATT_WIDTH), lambda i: (i, 0)),
            pl.BlockSpec((IN_TILE, SSM_WIDTH), lambda i: (i, 0)),
        ],
        out_shape=[
            jax.ShapeDtypeStruct((t, 3 * ATT_WIDTH), F32),
            jax.ShapeDtypeStruct((t, SSM_WIDTH), F32),
        ],
        compiler_params=pltpu.CompilerParams(
            dimension_semantics=("arbitrary",), vmem_limit_bytes=VMEM_LIMIT),
        name="inproj",
    )(x, g_mix.reshape(1, D_MODEL), w_in_bf)


def _t5_bucket_np(rel):
    half = NUM_BUCKETS // 2
    n = -rel
    ret = np.where(n < 0, half, 0)
    n = np.abs(n)
    max_exact = half // 2
    nf = np.maximum(n, 1).astype(np.float64)
    large = max_exact + (np.log(nf / max_exact) / math.log(REL_MAX_DISTANCE / max_exact)
                         * (half - max_exact)).astype(np.int64)
    large = np.minimum(large, half - 1)
    return ret + np.where(n < max_exact, n, large)


def _bucket_tables():
    qi = np.arange(Q_BLK)[:, None]
    ci = np.arange(K_BLK)[None, :]
    out = np.zeros((len(DIL_RATES), N_VARIANTS, Q_BLK, K_BLK), np.int32)
    for b, d in enumerate(DIL_RATES):
        for v, shift in enumerate((0, -RADIUS, -2 * RADIUS)):
            off = ci - qi + shift
            bk = _t5_bucket_np(off * d)
            out[b, v] = np.where(np.abs(off) <= RADIUS, bk, NUM_BUCKETS)
    return out.reshape(len(DIL_RATES) * N_VARIANTS, Q_BLK, K_BLK)


def _bias_kernel(rel_ref, bk_ref, tab_ref):
    hp = pl.program_id(0)
    n_bv = bk_ref.shape[0]
    for bv in range(n_bv):
        bk = bk_ref[bv]
        for h2 in range(HEADS_PER_STEP):
            acc = jnp.full(bk.shape, NEG_INF, F32)
            for b in range(NUM_BUCKETS):
                acc = jnp.where(bk == b, rel_ref[b, hp * HEADS_PER_STEP + h2], acc)
            tab_ref[0, bv * HEADS_PER_STEP + h2] = acc


def _bias_tables(rel_bias):
    bk = jnp.asarray(_bucket_tables())
    n_bv = bk.shape[0]
    n_hp = ATT_HEADS // HEADS_PER_STEP
    return pl.pallas_call(
        _bias_kernel,
        grid=(n_hp,),
        in_specs=[
            pl.BlockSpec(memory_space=pltpu.SMEM),
            pl.BlockSpec((n_bv, Q_BLK, K_BLK), lambda h: (0, 0, 0)),
        ],
        out_specs=pl.BlockSpec((1, n_bv * HEADS_PER_STEP, Q_BLK, K_BLK), lambda h: (h, 0, 0, 0)),
        out_shape=jax.ShapeDtypeStruct((n_hp, n_bv * HEADS_PER_STEP, Q_BLK, K_BLK), F32),
        compiler_params=pltpu.CompilerParams(dimension_semantics=("arbitrary",)),
        name="bias_tables",
    )(rel_bias, bk)


def _att_kernel(q_ref, k_ref, v_ref, tab_ref, o_ref, ob_scr, lse_scr, s_scr, p_scr, v_scr, *, seq_len):
    t = pl.program_id(2)
    lane = lax.broadcasted_iota(jnp.int32, (Q_BLK, LANES), 1)
    head0 = lane < HEAD_DIM
    n_sub = ATT_TILE // Q_BLK

    for b, d in enumerate(DIL_RATES):
        n_m = seq_len // d
        m_base = t * (ATT_TILE // d)

        def group(gi, carry, b=b, d=d, n_m=n_m, m_base=m_base):
            q_rows = []
            for j in range(ATT_GROUP):
                idx = gi * ATT_GROUP + j
                r = idx % d
                i = idx // d
                q_row = r + d * Q_BLK * i
                m0 = m_base + Q_BLK * i
                k_start = jnp.clip(m0 - RADIUS, 0, n_m - K_BLK)
                variant = jnp.where(m0 < RADIUS, 0, jnp.where(m0 > n_m - Q_BLK - RADIUS, 2, 1))
                k_row = r + d * k_start
                if d == 1:
                    q = q_ref[pl.ds(q_row, Q_BLK), :]
                    k = k_ref[pl.ds(k_row, K_BLK), :]
                    v = v_ref[pl.ds(k_row, K_BLK), :]
                else:
                    q = q_ref[pl.ds(q_row, Q_BLK, stride=d), :]
                    k = k_ref[pl.ds(k_row, K_BLK, stride=d), :]
                    v = v_ref[pl.ds(k_row, K_BLK, stride=d), :]
                q_rows.append(q_row)
                q = q * (HEAD_DIM ** -0.5)
                kb = k.astype(BF16)
                v_scr[j] = v.astype(BF16)
                for h2 in range(HEADS_PER_STEP):
                    sel = head0 if h2 == 0 else jnp.logical_not(head0)
                    qh = jnp.where(sel, q, 0.0).astype(BF16)
                    s = lax.dot_general(qh, kb, (((1,), (1,)), ((), ())), preferred_element_type=F32)
                    s_scr[j * HEADS_PER_STEP + h2] = (
                        s + tab_ref[0, (b * N_VARIANTS + variant) * HEADS_PER_STEP + h2])
            s_all = s_scr[...]
            m = jnp.max(s_all, axis=-1, keepdims=True)
            p = jnp.exp(s_all - m)
            den = jnp.sum(p, axis=-1, keepdims=True)
            p_scr[...] = p.astype(BF16)
            inv = 1.0 / den
            lse_all = m + jnp.log(den)
            for j in range(ATT_GROUP):
                outs = [jnp.dot(p_scr[j * HEADS_PER_STEP + h2], v_scr[j], preferred_element_type=F32)
                        * inv[j * HEADS_PER_STEP + h2] for h2 in range(HEADS_PER_STEP)]
                o = jnp.where(head0, outs[0], outs[1])
                lse = jnp.where(head0, lse_all[j * HEADS_PER_STEP], lse_all[j * HEADS_PER_STEP + 1])
                if d == 1:
                    ob_scr[b, pl.ds(q_rows[j], Q_BLK), :] = o
                    lse_scr[b, pl.ds(q_rows[j], Q_BLK), :] = lse
                else:
                    ob_scr.at[b][pl.ds(q_rows[j], Q_BLK, stride=d), :] = o
                    lse_scr.at[b][pl.ds(q_rows[j], Q_BLK, stride=d), :] = lse
            return carry

        lax.fori_loop(0, n_sub // ATT_GROUP, group, 0)

    lse_all = lse_scr[...]
    m = jnp.max(lse_all, axis=0)
    num = jnp.zeros((ATT_TILE, LANES), F32)
    den = jnp.zeros((ATT_TILE, LANES), F32)
    for b in range(len(DIL_RATES)):
        w = jnp.exp(lse_all[b] - m)
        num = num + w * ob_scr[b]
        den = den + w
    o_ref[...] = num / den


def _attention(qkv, tab, n_seq, seq_len):
    n_hp = ATT_HEADS // HEADS_PER_STEP
    n_t = seq_len // ATT_TILE
    n_br = len(DIL_RATES)
    return pl.pallas_call(
        functools.partial(_att_kernel, seq_len=seq_len),
        grid=(n_seq, n_hp, n_t),
        in_specs=[
            pl.BlockSpec((ATT_TILE, LANES), lambda s, h, t: (s * n_t + t, h)),
            pl.BlockSpec((seq_len, LANES), lambda s, h, t: (s, n_hp + h)),
            pl.BlockSpec((seq_len, LANES), lambda s, h, t: (s, 2 * n_hp + h)),
            pl.BlockSpec((1,) + tab.shape[1:], lambda s, h, t: (h, 0, 0, 0)),
        ],
        out_specs=pl.BlockSpec((ATT_TILE, LANES), lambda s, h, t: (s * n_t + t, h)),
        out_shape=jax.ShapeDtypeStruct((n_seq * seq_len, ATT_WIDTH), F32),
        scratch_shapes=[
            pltpu.VMEM((n_br, ATT_TILE, LANES), F32),
            pltpu.VMEM((n_br, ATT_TILE, LANES), F32),
            pltpu.VMEM((ATT_GROUP * HEADS_PER_STEP, Q_BLK, K_BLK), F32),
            pltpu.VMEM((ATT_GROUP * HEADS_PER_STEP, Q_BLK, K_BLK), BF16),
            pltpu.VMEM((ATT_GROUP, K_BLK, LANES), BF16),
        ],
        compiler_params=pltpu.CompilerParams(
            dimension_semantics=("arbitrary", "arbitrary", "arbitrary"),
            vmem_limit_bytes=VMEM_LIMIT),
        name="dilated_attention",
    )(qkv, qkv, qkv, tab)


def _ssm_weights(a_re, a_im, log_dt, b_re, b_im, c_re, c_im, d_skip):
    hi = lax.Precision.HIGHEST
    G, N, HC = SSM_GROUPS, SSM_STATE, SSM_GROUP_CH
    dt = jnp.exp(log_dt)[..., None]
    mag = jnp.exp(a_re * dt)
    ab_re = mag * jnp.cos(a_im * dt)
    ab_im = mag * jnp.sin(a_im * dt)
    inv = 1.0 / (a_re * a_re + a_im * a_im)
    f_re = ((ab_re - 1.0) * a_re + ab_im * a_im) * inv
    f_im = (ab_im * a_re - (ab_re - 1.0) * a_im) * inv
    bb_re = f_re[..., None] * b_re - f_im[..., None] * b_im
    bb_im = f_re[..., None] * b_im + f_im[..., None] * b_re
    prs, pis = [jnp.ones_like(ab_re)], [jnp.zeros_like(ab_re)]
    for _ in range(CHUNK):
        pr, pi = prs[-1], pis[-1]
        prs.append(pr * ab_re - pi * ab_im)
        pis.append(pr * ab_im + pi * ab_re)
    pr = jnp.stack(prs)
    pi = jnp.stack(pis)
    abr = pr[..., None] * bb_re - pi[..., None] * bb_im
    abi = pr[..., None] * bb_im + pi[..., None] * bb_re
    kern = (jnp.einsum('dgcn,tdgnk->tdgck', c_re, abr, precision=hi)
            - jnp.einsum('dgcn,tdgnk->tdgck', c_im, abi, precision=hi))

    jj = np.arange(CHUNK)[:, None]
    ii = np.arange(CHUNK)[None, :]
    lag_f = np.clip(ii - jj, 0, CHUNK - 1)
    lag_b = np.clip(jj - ii, 0, CHUNK - 1)
    kf = jnp.where((ii >= jj)[:, :, None, None, None], kern[:CHUNK, 0][lag_f], 0.0)
    kb = jnp.where((jj >= ii)[:, :, None, None, None], kern[:CHUNK, 1][lag_b], 0.0)
    m_tot = (kf + kb).transpose(2, 0, 4, 1, 3).reshape(G, CHUNK_COLS, CHUNK_COLS)

    def state_in(direction, rev):
        def rows(ab):
            sel = ab[:CHUNK, direction]
            if rev:
                sel = sel[::-1]
            return sel.transpose(1, 2, 0, 3).reshape(G, N, CHUNK_COLS)
        re, im = rows(abr), rows(abi)
        return jnp.concatenate([re, im, im, re], axis=1)

    w_cat = jnp.concatenate(
        [m_tot.transpose(0, 2, 1), state_in(0, True), state_in(1, False)], axis=1).astype(BF16)

    def state_out(direction, pr_sel, pi_sel):
        cr = c_re[direction][:, None, :, :]
        ci = c_im[direction][:, None, :, :]
        pr_s = pr_sel.transpose(1, 0, 2)[:, :, None, :]
        pi_s = pi_sel.transpose(1, 0, 2)[:, :, None, :]
        from_re = (cr * pr_s - ci * pi_s).reshape(G, CHUNK_COLS, N)
        from_im = (-cr * pi_s - ci * pr_s).reshape(G, CHUNK_COLS, N)
        return jnp.concatenate([from_re, from_im], axis=2)

    c_pow = jnp.concatenate(
        [state_out(0, pr[1:CHUNK + 1, 0], pi[1:CHUNK + 1, 0]),
         state_out(1, pr[CHUNK:0:-1, 1], pi[CHUNK:0:-1, 1])], axis=2).astype(BF16)

    def packed(direction):
        ar, ai = pr[CHUNK, direction], pi[CHUNK, direction]
        return [jnp.concatenate([ar, ar], -1), jnp.concatenate([-ai, ai], -1),
                jnp.concatenate([ai, -ai], -1)]
    a_chunk = jnp.stack(packed(0) + packed(1), axis=1)
    d_col = jnp.tile(d_skip[:, None, :], (1, CHUNK, 1)).reshape(G, CHUNK_COLS, 1)
    return w_cat, c_pow, a_chunk, d_col


def _to_chunk_major_kernel(u_ref, ut_ref):
    for j in range(CHUNK):
        ut_ref[j] = u_ref[pl.ds(j, LANES, stride=CHUNK), :].T


def _to_chunk_major(u):
    t = u.shape[0]
    tile = CHUNK * LANES
    return pl.pallas_call(
        _to_chunk_major_kernel,
        grid=(t // tile, SSM_WIDTH // LANES),
        in_specs=[pl.BlockSpec((tile, LANES), lambda i, l: (i, l))],
        out_specs=pl.BlockSpec((CHUNK, LANES, LANES), lambda i, l: (0, l, i)),
        out_shape=jax.ShapeDtypeStruct((CHUNK, SSM_WIDTH, t // CHUNK), F32),
        compiler_params=pltpu.CompilerParams(dimension_semantics=("arbitrary", "arbitrary")),
        name="to_chunk_major",
    )(u)


def _to_token_major_kernel(yt_ref, y_ref):
    for i in range(CHUNK):
        y_ref[pl.ds(i, LANES, stride=CHUNK), :] = yt_ref[i].T


def _to_token_major(yt):
    t = yt.shape[2] * CHUNK
    tile = CHUNK * LANES
    return pl.pallas_call(
        _to_token_major_kernel,
        grid=(t // tile, SSM_WIDTH // LANES),
        in_specs=[pl.BlockSpec((CHUNK, LANES, LANES), lambda i, l: (0, l, i))],
        out_specs=pl.BlockSpec((tile, LANES), lambda i, l: (i, l)),
        out_shape=jax.ShapeDtypeStruct((t, SSM_WIDTH), F32),
        compiler_params=pltpu.CompilerParams(dimension_semantics=("arbitrary", "arbitrary")),
        name="to_token_major",
    )(yt)


def _ssm_kernel(utp_ref, uts_ref, w_ref, cp_ref, a_ref, d_ref, ytp_ref, yts_ref, g_scr, h_scr,
                *, n_chunks, n_p, n_s):
    seqs = [(utp_ref, ytp_ref, s) for s in range(n_p)] + [(uts_ref, yts_ref, s) for s in range(n_s)]
    n_state = 2 * SSM_STATE

    @pl.when(pl.program_id(0) == 0)
    def _():
        g_scr[...] = jnp.zeros(g_scr.shape, F32)

    w = w_ref[0]
    for slot, (u_ref, y_ref, s) in enumerate(seqs):
        lanes = slice(s * n_chunks, (s + 1) * n_chunks)
        x_t = u_ref[:, :, lanes].reshape(CHUNK_COLS, n_chunks)
        r = jnp.dot(w, x_t.astype(BF16), preferred_element_type=F32)
        y_ref[:, :, lanes] = r[:CHUNK_COLS].reshape(CHUNK, SSM_GROUP_CH, n_chunks)
        for k in range(4):
            rows = slice(CHUNK_COLS + k * n_state, CHUNK_COLS + (k + 1) * n_state)
            g_scr.at[k][pl.ds(slot, n_chunks, stride=SEQ_PAD), :] = r[rows].T

    a = a_ref[0]
    shape = (SEQ_PAD, LANES)
    pf, qf, q2f, pb, qb, q2b = [jnp.broadcast_to(a[k:k + 1], shape) for k in range(6)]

    def step(c, carry):
        hf1, hf2, hb1, hb2 = carry
        rf = pl.ds(pl.multiple_of(c * SEQ_PAD, SEQ_PAD), SEQ_PAD)
        rb = pl.ds(pl.multiple_of((n_chunks - 1 - c) * SEQ_PAD, SEQ_PAD), SEQ_PAD)
        h_scr[0, rf, :] = hf1
        h_scr[1, rb, :] = hb1
        return (pf * hf1 + qf * hf2 + g_scr[0, rf, :], pf * hf2 + q2f * hf1 + g_scr[1, rf, :],
                pb * hb1 + qb * hb2 + g_scr[2, rb, :], pb * hb2 + q2b * hb1 + g_scr[3, rb, :])

    zero = jnp.zeros(shape, F32)
    lax.fori_loop(0, n_chunks, step, (zero, zero, zero, zero))

    cp = cp_ref[0]
    d_col = d_ref[0]
    for slot, (u_ref, y_ref, s) in enumerate(seqs):
        lanes = slice(s * n_chunks, (s + 1) * n_chunks)
        h_t = jnp.concatenate(
            [h_scr.at[k][pl.ds(slot, n_chunks, stride=SEQ_PAD), :].T for k in range(2)], axis=0)
        x_t = u_ref[:, :, lanes].reshape(CHUNK_COLS, n_chunks)
        y = (y_ref[:, :, lanes].reshape(CHUNK_COLS, n_chunks)
             + jnp.dot(cp, h_t.astype(BF16), preferred_element_type=F32) + d_col * x_t)
        y_ref[:, :, lanes] = y.reshape(CHUNK, SSM_GROUP_CH, n_chunks)


def _ssm(ut_p, ut_s, w_cat, c_pow, a_chunk, d_col, n_chunks):
    G = SSM_GROUPS
    n_p = ut_p.shape[2] // n_chunks
    n_s = ut_s.shape[2] // n_chunks
    n_rows = n_chunks * SEQ_PAD
    act = lambda arr: pl.BlockSpec((CHUNK, SSM_GROUP_CH, arr.shape[2]), lambda g: (0, g, 0))
    per_group = lambda arr: pl.BlockSpec((1,) + arr.shape[1:], lambda g: (g, 0, 0))
    return pl.pallas_call(
        functools.partial(_ssm_kernel, n_chunks=n_chunks, n_p=n_p, n_s=n_s),
        grid=(G,),
        in_specs=[act(ut_p), act(ut_s), per_group(w_cat), per_group(c_pow), per_group(a_chunk),
                  per_group(d_col)],
        out_specs=[act(ut_p), act(ut_s)],
        out_shape=[jax.ShapeDtypeStruct(ut_p.shape, F32), jax.ShapeDtypeStruct(ut_s.shape, F32)],
        scratch_shapes=[
            pltpu.VMEM((4, n_rows, LANES), F32),
            pltpu.VMEM((2, n_rows, LANES), F32),
        ],
        compiler_params=pltpu.CompilerParams(
            dimension_semantics=("arbitrary",), vmem_limit_bytes=VMEM_LIMIT),
        name="ssm_chunked",
    )(ut_p, ut_s, w_cat, c_pow, a_chunk, d_col)


def _ssm_mixer_pre(u_p, u_s, wts, n_chunks):
    yt_p, yt_s = _ssm(_to_chunk_major(u_p), _to_chunk_major(u_s), *wts, n_chunks)
    return _to_token_major(yt_p), _to_token_major(yt_s)


def _post_kernel(x_ref, att_ref, ys_ref, p_ref, wglu_ref, bglu_ref, gatt_ref, gssm_ref, wout_ref,
                 gmlp_ref, w1_ref, w2_ref, gple_ref, wgate_ref, wproj_ref, gfin_ref, o_ref):
    g = _gelu_tanh(ys_ref[...])
    ssm = g * _sigmoid(_bdot(g, wglu_ref[...]) + bglu_ref[...])
    att_n = _rms(att_ref[...], gatt_ref[...])
    ssm_n = _rms(ssm, gssm_ref[...])
    h = x_ref[...] + (_bdot(att_n, wout_ref[:ATT_WIDTH, :]) + _bdot(ssm_n, wout_ref[ATT_WIDTH:, :]))
    f = _rms(h, gmlp_ref[...]).astype(BF16)
    acc = jnp.zeros_like(h)
    for kb in range(D_FF // FF_BLK):
        cols = slice(kb * FF_BLK, (kb + 1) * FF_BLK)
        t = jnp.dot(f, w1_ref[:, cols], preferred_element_type=F32)
        t = jnp.square(jnp.maximum(t, 0.0))
        acc = acc + _bdot(t, w2_ref[cols, :])
    h = h + acc
    e = _rms(h, gple_ref[...])
    h = h + _sigmoid(_bdot(e, wgate_ref[...])) * _bdot(p_ref[...], wproj_ref[...])
    o_ref[...] = _rms(h, gfin_ref[...])


def _post(x, att, ys, p, wts):
    t = x.shape[0]
    tile = lambda width: pl.BlockSpec((POST_TILE, width), lambda i: (i, 0))

    def resident(arr):
        return pl.BlockSpec(arr.shape, lambda i: (0,) * arr.ndim, pipeline_mode=pl.Buffered(1))

    return pl.pallas_call(
        _post_kernel,
        grid=(t // POST_TILE,),
        in_specs=[tile(D_MODEL), tile(ATT_WIDTH), tile(SSM_WIDTH), tile(PLE_DIM)]
                 + [resident(w) for w in wts],
        out_specs=tile(D_MODEL),
        out_shape=jax.ShapeDtypeStruct((t, D_MODEL), F32),
        compiler_params=pltpu.CompilerParams(
            dimension_semantics=("arbitrary",), vmem_limit_bytes=VMEM_LIMIT),
        name="post_mixers",
    )(x, att, ys, p, *wts)


def kernel(x_prompt, x_sample, p_prompt, p_sample, rel_bias, g_mix, w_in, ssm_a_re, ssm_a_im, ssm_log_dt, ssm_b_re, ssm_b_im, ssm_c_re, ssm_c_im, ssm_d, w_glu, b_glu, g_att_out, g_ssm_out, w_out, g_mlp, w_mlp1, w_mlp2, g_ple, w_ple_gate, w_ple_proj, g_final):
    assert g_mix.shape[0] == 1, "single-layer trunk"
    seq_len = x_prompt.shape[1]
    assert x_sample.shape[1] == seq_len and seq_len % ATT_TILE == 0
    n_p, n_s = x_prompt.shape[0], x_sample.shape[0]
    assert n_p + n_s <= SEQ_PAD
    n_chunks = seq_len // CHUNK

    w_in_bf = w_in[0].astype(BF16)
    tab = _bias_tables(rel_bias)
    ssm_wts = _ssm_weights(
        ssm_a_re[0], ssm_a_im[0], ssm_log_dt[0], ssm_b_re[0], ssm_b_im[0],
        ssm_c_re[0], ssm_c_im[0], ssm_d[0])
    row = lambda v, n: v.reshape(1, n)
    wts = (w_glu[0].astype(BF16), row(b_glu[0], SSM_WIDTH), row(g_att_out[0], ATT_WIDTH),
           row(g_ssm_out[0], SSM_WIDTH), w_out[0].astype(BF16), row(g_mlp[0], D_MODEL),
           w_mlp1[0].astype(BF16), w_mlp2[0].astype(BF16), row(g_ple[0], D_MODEL),
           w_ple_gate[0].astype(BF16), w_ple_proj[0].astype(BF16), row(g_final, D_MODEL))

    groups = []
    for x3, p4 in ((x_prompt, p_prompt), (x_sample, p_sample)):
        n = x3.shape[0]
        x = x3.reshape(n * seq_len, D_MODEL)
        qkv, u = _inproj(x, g_mix[0], w_in_bf)
        att = _attention(qkv, tab, n, seq_len)
        groups.append((x, p4[0].reshape(n * seq_len, PLE_DIM), att, u))
    ys = _ssm_mixer_pre(groups[0][3], groups[1][3], ssm_wts, n_chunks)
    outs = [_post(x, att, y, p, wts).reshape(-1, seq_len, D_MODEL)
            for (x, p, att, _), y in zip(groups, ys)]
    return outs[0], outs[1]
```

```python
import functools
import math

import jax
import jax.numpy as jnp
import numpy as np
from jax import lax
from jax.experimental import pallas as pl
from jax.experimental.pallas import tpu as pltpu

F32 = jnp.float32
BF16 = jnp.bfloat16

D_MODEL = 1024
ATT_HEADS = 8
HEAD_DIM = 64
ATT_WIDTH = ATT_HEADS * HEAD_DIM
SSM_WIDTH = D_MODEL - ATT_WIDTH
SSM_GROUP_CH = 16
SSM_GROUPS = SSM_WIDTH // SSM_GROUP_CH
SSM_STATE = 64
IN_COLS = 3 * ATT_WIDTH + SSM_WIDTH
D_FF = 4 * D_MODEL
PLE_DIM = 256
NUM_BUCKETS = 32
REL_MAX_DISTANCE = 1024
DIL_WINDOWS = (128, 512, 2048)
DIL_RATES = (1, 4, 16)
RMS_EPS = 1e-6
NEG_INF = -1e30
LOG2_E = math.log2(math.e)

LANES = 128
SUBLANES = 8
VMEM_LIMIT = 56 * 1024 * 1024

RADIUS = 64
Q_BLK = 128
K_BLK = Q_BLK + 2 * RADIUS
ATT_TILE = 2048
HEADS_PER_STEP = LANES // HEAD_DIM
ATT_GROUP = 16
N_VARIANTS = 3

CHUNK = 16
CHUNK_COLS = CHUNK * SSM_GROUP_CH
SEQ_PAD = SUBLANES

IN_TILE = 512
POST_TILE = 512
FF_BLK = 1024


def _rms(x, g):
    return x * lax.rsqrt(jnp.mean(x * x, axis=-1, keepdims=True) + RMS_EPS) * g


def _sigmoid(x):
    return 1.0 / (1.0 + jnp.exp(-x))


def _gelu_tanh(x):
    c = math.sqrt(2.0 / math.pi)
    return 0.5 * x * (1.0 + jnp.tanh(c * (x + 0.044715 * (x * x * x))))


def _bdot(a, b):
    return jnp.dot(a.astype(BF16), b.astype(BF16), preferred_element_type=F32)


def _inproj_kernel(x_ref, g_ref, w_ref, qkv_ref, u_ref):
    a = _rms(x_ref[...], g_ref[...])
    z = _bdot(a, w_ref[...])
    qkv_ref[...] = z[:, :3 * ATT_WIDTH]
    u_ref[...] = z[:, 3 * ATT_WIDTH:]


def _inproj(x, g_mix, w_in_bf):
    t = x.shape[0]
    return pl.pallas_call(
        _inproj_kernel,
        grid=(t // IN_TILE,),
        in_specs=[
            pl.BlockSpec((IN_TILE, D_MODEL), lambda i: (i, 0)),
            pl.BlockSpec((1, D_MODEL), lambda i: (0, 0)),
            pl.BlockSpec((D_MODEL, IN_COLS), lambda i: (0, 0)),
        ],
        out_specs=[
            pl.BlockSpec((IN_TILE, 3 * ATT_WIDTH), lambda i: (i, 0)),
            pl.BlockSpec((IN_TILE, SSM_WIDTH), lambda i: (i, 0)),
        ],
        out_shape=[
            jax.ShapeDtypeStruct((t, 3 * ATT_WIDTH), F32),
            jax.ShapeDtypeStruct((t, SSM_WIDTH), F32),
        ],
        compiler_params=pltpu.CompilerParams(
            dimension_semantics=("arbitrary",), vmem_limit_bytes=VMEM_LIMIT),
        name="inproj",
    )(x, g_mix.reshape(1, D_MODEL), w_in_bf)


def _t5_bucket_np(rel):
    half = NUM_BUCKETS // 2
    n = -rel
    ret = np.where(n < 0, half, 0)
    n = np.abs(n)
    max_exact = half // 2
    nf = np.maximum(n, 1).astype(np.float64)
    large = max_exact + (np.log(nf / max_exact) / math.log(REL_MAX_DISTANCE / max_exact)
                         * (half - max_exact)).astype(np.int64)
    large = np.minimum(large, half - 1)
    return ret + np.where(n < max_exact, n, large)


def _bucket_tables():
    qi = np.arange(Q_BLK)[:, None]
    ci = np.arange(K_BLK)[None, :]
    out = np.zeros((len(DIL_RATES), N_VARIANTS, Q_BLK, K_BLK), np.int32)
    for b, d in enumerate(DIL_RATES):
        for v, shift in enumerate((0, -RADIUS, -2 * RADIUS)):
            off = ci - qi + shift
            bk = _t5_bucket_np(off * d)
            out[b, v] = np.where(np.abs(off) <= RADIUS, bk, NUM_BUCKETS)
    return out.reshape(len(DIL_RATES) * N_VARIANTS, Q_BLK, K_BLK)


def _bias_kernel(rel_ref, bk_ref, tab_ref):
    hp = pl.program_id(0)
    n_bv = bk_ref.shape[0]
    for bv in range(n_bv):
        bk = bk_ref[bv]
        for h2 in range(HEADS_PER_STEP):
            acc = jnp.full(bk.shape, NEG_INF, F32)
            for b in range(NUM_BUCKETS):
                acc = jnp.where(bk == b, rel_ref[b, hp * HEADS_PER_STEP + h2] * LOG2_E, acc)
            tab_ref[0, bv * HEADS_PER_STEP + h2] = acc


def _bias_tables(rel_bias):
    bk = jnp.asarray(_bucket_tables())
    n_bv = bk.shape[0]
    n_hp = ATT_HEADS // HEADS_PER_STEP
    return pl.pallas_call(
        _bias_kernel,
        grid=(n_hp,),
        in_specs=[
            pl.BlockSpec(memory_space=pltpu.SMEM),
            pl.BlockSpec((n_bv, Q_BLK, K_BLK), lambda h: (0, 0, 0)),
        ],
        out_specs=pl.BlockSpec((1, n_bv * HEADS_PER_STEP, Q_BLK, K_BLK), lambda h: (h, 0, 0, 0)),
        out_shape=jax.ShapeDtypeStruct((n_hp, n_bv * HEADS_PER_STEP, Q_BLK, K_BLK), F32),
        compiler_params=pltpu.CompilerParams(dimension_semantics=("arbitrary",)),
        name="bias_tables",
    )(rel_bias, bk)


def _att_kernel(q_ref, k_ref, v_ref, tab_ref, o_ref, acc_scr, m_scr, den_scr, s_scr, p_scr, v_scr,
                *, seq_len):
    t = pl.program_id(2)
    lane = lax.broadcasted_iota(jnp.int32, (Q_BLK, LANES), 1)
    head0 = lane < HEAD_DIM
    n_sub = ATT_TILE // Q_BLK

    def strided(ref, row, n, d):
        return ref[pl.ds(row, n), :] if d == 1 else ref[pl.ds(row, n, stride=d), :]

    def strided_store(ref, row, d, val):
        if d == 1:
            ref[pl.ds(row, Q_BLK), :] = val
        else:
            ref[pl.ds(row, Q_BLK, stride=d), :] = val

    for b, d in enumerate(DIL_RATES):
        n_m = seq_len // d
        m_base = t * (ATT_TILE // d)
        for g0 in range(0, n_sub, ATT_GROUP):
            q_rows = []
            for j in range(ATT_GROUP):
                idx = g0 + j
                r, i = idx % d, idx // d
                q_row = r + d * Q_BLK * i
                m0 = m_base + Q_BLK * i
                k_start = jnp.clip(m0 - RADIUS, 0, n_m - K_BLK)
                variant = jnp.where(m0 < RADIUS, 0, jnp.where(m0 > n_m - Q_BLK - RADIUS, 2, 1))
                k_row = r + d * k_start
                q = strided(q_ref, q_row, Q_BLK, d) * (LOG2_E * HEAD_DIM ** -0.5)
                kb = strided(k_ref, k_row, K_BLK, d).astype(BF16)
                v_scr[j] = strided(v_ref, k_row, K_BLK, d).astype(BF16)
                q_rows.append(q_row)
                for h2 in range(HEADS_PER_STEP):
                    sel = head0 if h2 == 0 else jnp.logical_not(head0)
                    qh = jnp.where(sel, q, 0.0).astype(BF16)
                    s = lax.dot_general(qh, kb, (((1,), (1,)), ((), ())), preferred_element_type=F32)
                    s_scr[j * HEADS_PER_STEP + h2] = (
                        s + tab_ref[0, (b * N_VARIANTS + variant) * HEADS_PER_STEP + h2])
            for j in range(ATT_GROUP):
                ms, dens = [], []
                for h2 in range(HEADS_PER_STEP):
                    s = s_scr[j * HEADS_PER_STEP + h2]
                    m = jnp.max(s, axis=-1, keepdims=True)
                    p = jnp.exp2(s - m)
                    p_scr[j * HEADS_PER_STEP + h2] = p.astype(BF16)
                    ms.append(m)
                    dens.append(jnp.sum(p, axis=-1, keepdims=True))
                strided_store(m_scr.at[b], q_rows[j], d, jnp.where(head0, ms[0], ms[1]))
                strided_store(den_scr.at[b], q_rows[j], d, jnp.where(head0, dens[0], dens[1]))
            for j in range(ATT_GROUP):
                outs = [jnp.dot(p_scr[j * HEADS_PER_STEP + h2], v_scr[j], preferred_element_type=F32)
                        for h2 in range(HEADS_PER_STEP)]
                strided_store(acc_scr.at[b], q_rows[j], d, jnp.where(head0, outs[0], outs[1]))

    m_all = m_scr[...]
    m = jnp.max(m_all, axis=0)
    num = jnp.zeros((ATT_TILE, LANES), F32)
    den = jnp.zeros((ATT_TILE, LANES), F32)
    for b in range(len(DIL_RATES)):
        w = jnp.exp2(m_all[b] - m)
        num = num + w * acc_scr[b]
        den = den + w * den_scr[b]
    o_ref[...] = num / den


def _attention(qkv, tab, n_seq, seq_len):
    n_hp = ATT_HEADS // HEADS_PER_STEP
    n_t = seq_len // ATT_TILE
    n_br = len(DIL_RATES)
    return pl.pallas_call(
        functools.partial(_att_kernel, seq_len=seq_len),
        grid=(n_seq, n_hp, n_t),
        in_specs=[
            pl.BlockSpec((ATT_TILE, LANES), lambda s, h, t: (s * n_t + t, h)),
            pl.BlockSpec((seq_len, LANES), lambda s, h, t: (s, n_hp + h)),
            pl.BlockSpec((seq_len, LANES), lambda s, h, t: (s, 2 * n_hp + h)),
            pl.BlockSpec((1,) + tab.shape[1:], lambda s, h, t: (h, 0, 0, 0)),
        ],
        out_specs=pl.BlockSpec((ATT_TILE, LANES), lambda s, h, t: (s * n_t + t, h)),
        out_shape=jax.ShapeDtypeStruct((n_seq * seq_len, ATT_WIDTH), F32),
        scratch_shapes=[
            pltpu.VMEM((n_br, ATT_TILE, LANES), F32),
            pltpu.VMEM((n_br, ATT_TILE, LANES), F32),
            pltpu.VMEM((n_br, ATT_TILE, LANES), F32),
            pltpu.VMEM((ATT_GROUP * HEADS_PER_STEP, Q_BLK, K_BLK), F32),
            pltpu.VMEM((ATT_GROUP * HEADS_PER_STEP, Q_BLK, K_BLK), BF16),
            pltpu.VMEM((ATT_GROUP, K_BLK, LANES), BF16),
        ],
        compiler_params=pltpu.CompilerParams(
            dimension_semantics=("arbitrary", "arbitrary", "arbitrary"),
            vmem_limit_bytes=VMEM_LIMIT),
        name="dilated_attention",
    )(qkv, qkv, qkv, tab)


def _ssm_weights(a_re, a_im, log_dt, b_re, b_im, c_re, c_im, d_skip):
    hi = lax.Precision.HIGHEST
    G, N, HC = SSM_GROUPS, SSM_STATE, SSM_GROUP_CH
    dt = jnp.exp(log_dt)[..., None]
    mag = jnp.exp(a_re * dt)
    ab_re = mag * jnp.cos(a_im * dt)
    ab_im = mag * jnp.sin(a_im * dt)
    inv = 1.0 / (a_re * a_re + a_im * a_im)
    f_re = ((ab_re - 1.0) * a_re + ab_im * a_im) * inv
    f_im = (ab_im * a_re - (ab_re - 1.0) * a_im) * inv
    bb_re = f_re[..., None] * b_re - f_im[..., None] * b_im
    bb_im = f_re[..., None] * b_im + f_im[..., None] * b_re
    prs, pis = [jnp.ones_like(ab_re)], [jnp.zeros_like(ab_re)]
    for _ in range(CHUNK):
        pr, pi = prs[-1], pis[-1]
        prs.append(pr * ab_re - pi * ab_im)
        pis.append(pr * ab_im + pi * ab_re)
    pr = jnp.stack(prs)
    pi = jnp.stack(pis)
    abr = pr[..., None] * bb_re - pi[..., None] * bb_im
    abi = pr[..., None] * bb_im + pi[..., None] * bb_re
    kern = (jnp.einsum('dgcn,tdgnk->tdgck', c_re, abr, precision=hi)
            - jnp.einsum('dgcn,tdgnk->tdgck', c_im, abi, precision=hi))

    jj = np.arange(CHUNK)[:, None]
    ii = np.arange(CHUNK)[None, :]
    lag_f = np.clip(ii - jj, 0, CHUNK - 1)
    lag_b = np.clip(jj - ii, 0, CHUNK - 1)
    kf = jnp.where((ii >= jj)[:, :, None, None, None], kern[:CHUNK, 0][lag_f], 0.0)
    kb = jnp.where((jj >= ii)[:, :, None, None, None], kern[:CHUNK, 1][lag_b], 0.0)
    m_tot = (kf + kb).transpose(2, 0, 4, 1, 3).reshape(G, CHUNK_COLS, CHUNK_COLS)

    def state_in(direction, rev):
        def rows(ab):
            sel = ab[:CHUNK, direction]
            if rev:
                sel = sel[::-1]
            return sel.transpose(1, 2, 0, 3).reshape(G, N, CHUNK_COLS)
        re, im = rows(abr), rows(abi)
        return jnp.concatenate([re, im, im, re], axis=1)

    w_cat = jnp.concatenate(
        [m_tot.transpose(0, 2, 1), state_in(0, True), state_in(1, False)], axis=1).astype(BF16)

    def state_out(direction, pr_sel, pi_sel):
        cr = c_re[direction][:, None, :, :]
        ci = c_im[direction][:, None, :, :]
        pr_s = pr_sel.transpose(1, 0, 2)[:, :, None, :]
        pi_s = pi_sel.transpose(1, 0, 2)[:, :, None, :]
        from_re = (cr * pr_s - ci * pi_s).reshape(G, CHUNK_COLS, N)
        from_im = (-cr * pi_s - ci * pr_s).reshape(G, CHUNK_COLS, N)
        return jnp.concatenate([from_re, from_im], axis=2)

    c_pow = jnp.concatenate(
        [state_out(0, pr[1:CHUNK + 1, 0], pi[1:CHUNK + 1, 0]),
         state_out(1, pr[CHUNK:0:-1, 1], pi[CHUNK:0:-1, 1])], axis=2).astype(BF16)

    def packed(direction):
        ar, ai = pr[CHUNK, direction], pi[CHUNK, direction]
        return [jnp.concatenate([ar, ar], -1), jnp.concatenate([-ai, ai], -1),
                jnp.concatenate([ai, -ai], -1)]
    a_chunk = jnp.stack(packed(0) + packed(1), axis=1)
    d_col = jnp.tile(d_skip[:, None, :], (1, CHUNK, 1)).reshape(G, CHUNK_COLS, 1)
    return w_cat, c_pow, a_chunk, d_col


def _to_chunk_major_kernel(u_ref, ut_ref):
    for j in range(CHUNK):
        ut_ref[j] = u_ref[pl.ds(j, LANES, stride=CHUNK), :].T


def _to_chunk_major(u):
    t = u.shape[0]
    tile = CHUNK * LANES
    return pl.pallas_call(
        _to_chunk_major_kernel,
        grid=(t // tile, SSM_WIDTH // LANES),
        in_specs=[pl.BlockSpec((tile, LANES), lambda i, l: (i, l))],
        out_specs=pl.BlockSpec((CHUNK, LANES, LANES), lambda i, l: (0, l, i)),
        out_shape=jax.ShapeDtypeStruct((CHUNK, SSM_WIDTH, t // CHUNK), F32),
        compiler_params=pltpu.CompilerParams(dimension_semantics=("arbitrary", "arbitrary")),
        name="to_chunk_major",
    )(u)


def _to_token_major_kernel(yt_ref, y_ref):
    for i in range(CHUNK):
        y_ref[pl.ds(i, LANES, stride=CHUNK), :] = yt_ref[i].T


def _to_token_major(yt):
    t = yt.shape[2] * CHUNK
    tile = CHUNK * LANES
    return pl.pallas_call(
        _to_token_major_kernel,
        grid=(t // tile, SSM_WIDTH // LANES),
        in_specs=[pl.BlockSpec((CHUNK, LANES, LANES), lambda i, l: (0, l, i))],
        out_specs=pl.BlockSpec((tile, LANES), lambda i, l: (i, l)),
        out_shape=jax.ShapeDtypeStruct((t, SSM_WIDTH), F32),
        compiler_params=pltpu.CompilerParams(dimension_semantics=("arbitrary", "arbitrary")),
        name="to_token_major",
    )(yt)


def _ssm_kernel(utp_ref, uts_ref, w_ref, cp_ref, a_ref, d_ref, ytp_ref, yts_ref, g_scr, h_scr,
                *, n_chunks, n_p, n_s):
    seqs = [(utp_ref, ytp_ref, s) for s in range(n_p)] + [(uts_ref, yts_ref, s) for s in range(n_s)]
    n_state = 2 * SSM_STATE

    @pl.when(pl.program_id(0) == 0)
    def _():
        g_scr[...] = jnp.zeros(g_scr.shape, F32)

    w = w_ref[0]
    for slot, (u_ref, y_ref, s) in enumerate(seqs):
        lanes = slice(s * n_chunks, (s + 1) * n_chunks)
        x_t = u_ref[:, :, lanes].reshape(CHUNK_COLS, n_chunks)
        r = jnp.dot(w, x_t.astype(BF16), preferred_element_type=F32)
        y_ref[:, :, lanes] = r[:CHUNK_COLS].reshape(CHUNK, SSM_GROUP_CH, n_chunks)
        for k in range(4):
            rows = slice(CHUNK_COLS + k * n_state, CHUNK_COLS + (k + 1) * n_state)
            g_scr.at[k][pl.ds(slot, n_chunks, stride=SEQ_PAD), :] = r[rows].T

    a = a_ref[0]
    shape = (SEQ_PAD, LANES)
    pf, qf, q2f, pb, qb, q2b = [jnp.broadcast_to(a[k:k + 1], shape) for k in range(6)]

    def step(c, carry):
        hf1, hf2, hb1, hb2 = carry
        rf = pl.ds(pl.multiple_of(c * SEQ_PAD, SEQ_PAD), SEQ_PAD)
        rb = pl.ds(pl.multiple_of((n_chunks - 1 - c) * SEQ_PAD, SEQ_PAD), SEQ_PAD)
        h_scr[0, rf, :] = hf1
        h_scr[1, rb, :] = hb1
        return (pf * hf1 + qf * hf2 + g_scr[0, rf, :], pf * hf2 + q2f * hf1 + g_scr[1, rf, :],
                pb * hb1 + qb * hb2 + g_scr[2, rb, :], pb * hb2 + q2b * hb1 + g_scr[3, rb, :])

    zero = jnp.zeros(shape, F32)
    lax.fori_loop(0, n_chunks, step, (zero, zero, zero, zero))

    cp = cp_ref[0]
    d_col = d_ref[0]
    for slot, (u_ref, y_ref, s) in enumerate(seqs):
        lanes = slice(s * n_chunks, (s + 1) * n_chunks)
        h_t = jnp.concatenate(
            [h_scr.at[k][pl.ds(slot, n_chunks, stride=SEQ_PAD), :].T for k in range(2)], axis=0)
        x_t = u_ref[:, :, lanes].reshape(CHUNK_COLS, n_chunks)
        y = (y_ref[:, :, lanes].reshape(CHUNK_COLS, n_chunks)
             + jnp.dot(cp, h_t.astype(BF16), preferred_element_type=F32) + d_col * x_t)
        y_ref[:, :, lanes] = y.reshape(CHUNK, SSM_GROUP_CH, n_chunks)


def _ssm(ut_p, ut_s, w_cat, c_pow, a_chunk, d_col, n_chunks):
    G = SSM_GROUPS
    n_p = ut_p.shape[2] // n_chunks
    n_s = ut_s.shape[2] // n_chunks
    n_rows = n_chunks * SEQ_PAD
    act = lambda arr: pl.BlockSpec((CHUNK, SSM_GROUP_CH, arr.shape[2]), lambda g: (0, g, 0))
    per_group = lambda arr: pl.BlockSpec((1,) + arr.shape[1:], lambda g: (g, 0, 0))
    return pl.pallas_call(
        functools.partial(_ssm_kernel, n_chunks=n_chunks, n_p=n_p, n_s=n_s),
        grid=(G,),
        in_specs=[act(ut_p), act(ut_s), per_group(w_cat), per_group(c_pow), per_group(a_chunk),
                  per_group(d_col)],
        out_specs=[act(ut_p), act(ut_s)],
        out_shape=[jax.ShapeDtypeStruct(ut_p.shape, F32), jax.ShapeDtypeStruct(ut_s.shape, F32)],
        scratch_shapes=[
            pltpu.VMEM((4, n_rows, LANES), F32),
            pltpu.VMEM((2, n_rows, LANES), F32),
        ],
        compiler_params=pltpu.CompilerParams(
            dimension_semantics=("arbitrary",), vmem_limit_bytes=VMEM_LIMIT),
        name="ssm_chunked",
    )(ut_p, ut_s, w_cat, c_pow, a_chunk, d_col)


def _ssm_mixer_pre(u_p, u_s, wts, n_chunks):
    yt_p, yt_s = _ssm(_to_chunk_major(u_p), _to_chunk_major(u_s), *wts, n_chunks)
    return _to_token_major(yt_p), _to_token_major(yt_s)


def _post_kernel(x_ref, att_ref, ys_ref, p_ref, wglu_ref, bglu_ref, gatt_ref, gssm_ref, wout_ref,
                 gmlp_ref, w1_ref, w2_ref, gple_ref, wgate_ref, wproj_ref, gfin_ref, o_ref):
    g = _gelu_tanh(ys_ref[...])
    ssm = g * _sigmoid(_bdot(g, wglu_ref[...]) + bglu_ref[...])
    att_n = _rms(att_ref[...], gatt_ref[...])
    ssm_n = _rms(ssm, gssm_ref[...])
    h = x_ref[...] + (_bdot(att_n, wout_ref[:ATT_WIDTH, :]) + _bdot(ssm_n, wout_ref[ATT_WIDTH:, :]))
    f = _rms(h, gmlp_ref[...]).astype(BF16)
    acc = jnp.zeros_like(h)
    for kb in range(D_FF // FF_BLK):
        cols = slice(kb * FF_BLK, (kb + 1) * FF_BLK)
        t = jnp.dot(f, w1_ref[:, cols], preferred_element_type=F32)
        t = jnp.square(jnp.maximum(t, 0.0))
        acc = acc + _bdot(t, w2_ref[cols, :])
    h = h + acc
    e = _rms(h, gple_ref[...])
    h = h + _sigmoid(_bdot(e, wgate_ref[...])) * _bdot(p_ref[...], wproj_ref[...])
    o_ref[...] = _rms(h, gfin_ref[...])


def _post(x, att, ys, p, wts):
    t = x.shape[0]
    tile = lambda width: pl.BlockSpec((POST_TILE, width), lambda i: (i, 0))

    def resident(arr):
        return pl.BlockSpec(arr.shape, lambda i: (0,) * arr.ndim, pipeline_mode=pl.Buffered(1))

    return pl.pallas_call(
        _post_kernel,
        grid=(t // POST_TILE,),
        in_specs=[tile(D_MODEL), tile(ATT_WIDTH), tile(SSM_WIDTH), tile(PLE_DIM)]
                 + [resident(w) for w in wts],
        out_specs=tile(D_MODEL),
        out_shape=jax.ShapeDtypeStruct((t, D_MODEL), F32),
        compiler_params=pltpu.CompilerParams(
            dimension_semantics=("arbitrary",), vmem_limit_bytes=VMEM_LIMIT),
        name="post_mixers",
    )(x, att, ys, p, *wts)


def kernel(x_prompt, x_sample, p_prompt, p_sample, rel_bias, g_mix, w_in, ssm_a_re, ssm_a_im, ssm_log_dt, ssm_b_re, ssm_b_im, ssm_c_re, ssm_c_im, ssm_d, w_glu, b_glu, g_att_out, g_ssm_out, w_out, g_mlp, w_mlp1, w_mlp2, g_ple, w_ple_gate, w_ple_proj, g_final):
    assert g_mix.shape[0] == 1, "single-layer trunk"
    seq_len = x_prompt.shape[1]
    assert x_sample.shape[1] == seq_len and seq_len % ATT_TILE == 0
    n_p, n_s = x_prompt.shape[0], x_sample.shape[0]
    assert n_p + n_s <= SEQ_PAD
    n_chunks = seq_len // CHUNK

    w_in_bf = w_in[0].astype(BF16)
    tab = _bias_tables(rel_bias)
    ssm_wts = _ssm_weights(
        ssm_a_re[0], ssm_a_im[0], ssm_log_dt[0], ssm_b_re[0], ssm_b_im[0],
        ssm_c_re[0], ssm_c_im[0], ssm_d[0])
    row = lambda v, n: v.reshape(1, n)
    wts = (w_glu[0].astype(BF16), row(b_glu[0], SSM_WIDTH), row(g_att_out[0], ATT_WIDTH),
           row(g_ssm_out[0], SSM_WIDTH), w_out[0].astype(BF16), row(g_mlp[0], D_MODEL),
           w_mlp1[0].astype(BF16), w_mlp2[0].astype(BF16), row(g_ple[0], D_MODEL),
           w_ple_gate[0].astype(BF16), w_ple_proj[0].astype(BF16), row(g_final, D_MODEL))

    groups = []
    for x3, p4 in ((x_prompt, p_prompt), (x_sample, p_sample)):
        n = x3.shape[0]
        x = x3.reshape(n * seq_len, D_MODEL)
        qkv, u = _inproj(x, g_mix[0], w_in_bf)
        att = _attention(qkv, tab, n, seq_len)
        groups.append((x, p4[0].reshape(n * seq_len, PLE_DIM), att, u))
    ys = _ssm_mixer_pre(groups[0][3], groups[1][3], ssm_wts, n_chunks)
    outs = [_post(x, att, y, p, wts).reshape(-1, seq_len, D_MODEL)
            for (x, p, att, _), y in zip(groups, ys)]
    return outs[0], outs[1]
```

```python
import functools
import math

import jax
import jax.numpy as jnp
import numpy as np
from jax import lax
from jax.experimental import pallas as pl
from jax.experimental.pallas import tpu as pltpu

F32 = jnp.float32
BF16 = jnp.bfloat16

D_MODEL = 1024
ATT_HEADS = 8
HEAD_DIM = 64
ATT_WIDTH = ATT_HEADS * HEAD_DIM
SSM_WIDTH = D_MODEL - ATT_WIDTH
SSM_GROUP_CH = 16
SSM_GROUPS = SSM_WIDTH // SSM_GROUP_CH
SSM_STATE = 64
IN_COLS = 3 * ATT_WIDTH + SSM_WIDTH
D_FF = 4 * D_MODEL
PLE_DIM = 256
NUM_BUCKETS = 32
REL_MAX_DISTANCE = 1024
DIL_WINDOWS = (128, 512, 2048)
DIL_RATES = (1, 4, 16)
RMS_EPS = 1e-6
NEG_INF = -1e30
LOG2_E = math.log2(math.e)

LANES = 128
SUBLANES = 8
VMEM_LIMIT = 56 * 1024 * 1024

RADIUS = 64
Q_BLK = 128
K_BLK = Q_BLK + 2 * RADIUS
ATT_TILE = 2048
HEADS_PER_STEP = LANES // HEAD_DIM
ATT_GROUP = 16
N_VARIANTS = 3

CHUNK = 16
CHUNK_COLS = CHUNK * SSM_GROUP_CH
SEQ_PAD = SUBLANES

SUPER_TILE = CHUNK * LANES
IN_TILE = 512
POST_TILE = 512
FF_BLK = 1024


def _rms(x, g):
    return x * lax.rsqrt(jnp.mean(x * x, axis=-1, keepdims=True) + RMS_EPS) * g


def _sigmoid(x):
    return 1.0 / (1.0 + jnp.exp(-x))


def _gelu_tanh(x):
    c = math.sqrt(2.0 / math.pi)
    return 0.5 * x * (1.0 + jnp.tanh(c * (x + 0.044715 * (x * x * x))))


def _bdot(a, b):
    return jnp.dot(a.astype(BF16), b.astype(BF16), preferred_element_type=F32)


def _inproj_kernel(x_ref, g_ref, w_ref, qkv_ref, ut_ref, u_scr):
    k = pl.program_id(1)
    a = _rms(x_ref[...], g_ref[...])
    z = _bdot(a, w_ref[...])
    qkv_ref[...] = z[:, :3 * ATT_WIDTH]
    rows = pl.ds(pl.multiple_of(k * IN_TILE, IN_TILE), IN_TILE)
    for l in range(SSM_WIDTH // LANES):
        u_scr[l, rows, :] = z[:, 3 * ATT_WIDTH + l * LANES:3 * ATT_WIDTH + (l + 1) * LANES]

    @pl.when(k == SUPER_TILE // IN_TILE - 1)
    def _():
        for l in range(SSM_WIDTH // LANES):
            for j in range(CHUNK):
                ut_ref[j, l * LANES:(l + 1) * LANES, :] = u_scr.at[l][pl.ds(j, LANES, stride=CHUNK), :].T


def _inproj(x, g_mix, w_in_bf):
    t = x.shape[0]
    n_in = SUPER_TILE // IN_TILE
    return pl.pallas_call(
        _inproj_kernel,
        grid=(t // SUPER_TILE, n_in),
        in_specs=[
            pl.BlockSpec((IN_TILE, D_MODEL), lambda i, k: (i * n_in + k, 0)),
            pl.BlockSpec((1, D_MODEL), lambda i, k: (0, 0)),
            pl.BlockSpec((D_MODEL, IN_COLS), lambda i, k: (0, 0)),
        ],
        out_specs=[
            pl.BlockSpec((IN_TILE, 3 * ATT_WIDTH), lambda i, k: (i * n_in + k, 0)),
            pl.BlockSpec((CHUNK, SSM_WIDTH, LANES), lambda i, k: (0, 0, i)),
        ],
        out_shape=[
            jax.ShapeDtypeStruct((t, 3 * ATT_WIDTH), F32),
            jax.ShapeDtypeStruct((CHUNK, SSM_WIDTH, t // CHUNK), F32),
        ],
        scratch_shapes=[pltpu.VMEM((SSM_WIDTH // LANES, SUPER_TILE, LANES), F32)],
        compiler_params=pltpu.CompilerParams(
            dimension_semantics=("arbitrary", "arbitrary"), vmem_limit_bytes=VMEM_LIMIT),
        name="inproj",
    )(x, g_mix.reshape(1, D_MODEL), w_in_bf)


def _t5_bucket_np(rel):
    half = NUM_BUCKETS // 2
    n = -rel
    ret = np.where(n < 0, half, 0)
    n = np.abs(n)
    max_exact = half // 2
    nf = np.maximum(n, 1).astype(np.float64)
    large = max_exact + (np.log(nf / max_exact) / math.log(REL_MAX_DISTANCE / max_exact)
                         * (half - max_exact)).astype(np.int64)
    large = np.minimum(large, half - 1)
    return ret + np.where(n < max_exact, n, large)


def _bucket_tables():
    qi = np.arange(Q_BLK)[:, None]
    ci = np.arange(K_BLK)[None, :]
    out = np.zeros((len(DIL_RATES), N_VARIANTS, Q_BLK, K_BLK), np.int32)
    for b, d in enumerate(DIL_RATES):
        for v, shift in enumerate((0, -RADIUS, -2 * RADIUS)):
            off = ci - qi + shift
            bk = _t5_bucket_np(off * d)
            out[b, v] = np.where(np.abs(off) <= RADIUS, bk, NUM_BUCKETS)
    return out.reshape(len(DIL_RATES) * N_VARIANTS, Q_BLK, K_BLK)


def _bias_kernel(rel_ref, bk_ref, tab_ref):
    hp = pl.program_id(0)
    n_bv = bk_ref.shape[0]
    for bv in range(n_bv):
        bk = bk_ref[bv]
        for h2 in range(HEADS_PER_STEP):
            acc = jnp.full(bk.shape, NEG_INF, F32)
            for b in range(NUM_BUCKETS):
                acc = jnp.where(bk == b, rel_ref[b, hp * HEADS_PER_STEP + h2] * LOG2_E, acc)
            tab_ref[0, bv * HEADS_PER_STEP + h2] = acc


def _bias_tables(rel_bias):
    bk = jnp.asarray(_bucket_tables())
    n_bv = bk.shape[0]
    n_hp = ATT_HEADS // HEADS_PER_STEP
    return pl.pallas_call(
        _bias_kernel,
        grid=(n_hp,),
        in_specs=[
            pl.BlockSpec(memory_space=pltpu.SMEM),
            pl.BlockSpec((n_bv, Q_BLK, K_BLK), lambda h: (0, 0, 0)),
        ],
        out_specs=pl.BlockSpec((1, n_bv * HEADS_PER_STEP, Q_BLK, K_BLK), lambda h: (h, 0, 0, 0)),
        out_shape=jax.ShapeDtypeStruct((n_hp, n_bv * HEADS_PER_STEP, Q_BLK, K_BLK), F32),
        compiler_params=pltpu.CompilerParams(dimension_semantics=("arbitrary",)),
        name="bias_tables",
    )(rel_bias, bk)


def _att_kernel(q_ref, k_ref, v_ref, tab_ref, o_ref, acc_scr, m_scr, den_scr, s_scr, p_scr, v_scr,
                *, seq_len):
    t = pl.program_id(2)
    lane = lax.broadcasted_iota(jnp.int32, (Q_BLK, LANES), 1)
    head0 = lane < HEAD_DIM
    head0_k = lax.broadcasted_iota(jnp.int32, (K_BLK, LANES), 1) < HEAD_DIM
    n_sub = ATT_TILE // Q_BLK

    def strided(ref, row, n, d):
        return ref[pl.ds(row, n), :] if d == 1 else ref[pl.ds(row, n, stride=d), :]

    def strided_store(ref, row, d, val):
        if d == 1:
            ref[pl.ds(row, Q_BLK), :] = val
        else:
            ref[pl.ds(row, Q_BLK, stride=d), :] = val

    for b, d in enumerate(DIL_RATES):
        n_m = seq_len // d
        m_base = t * (ATT_TILE // d)
        for g0 in range(0, n_sub, ATT_GROUP):
            q_rows = []
            for j in range(ATT_GROUP):
                idx = g0 + j
                r, i = idx % d, idx // d
                q_row = r + d * Q_BLK * i
                m0 = m_base + Q_BLK * i
                k_start = jnp.clip(m0 - RADIUS, 0, n_m - K_BLK)
                variant = jnp.where(m0 < RADIUS, 0, jnp.where(m0 > n_m - Q_BLK - RADIUS, 2, 1))
                k_row = r + d * k_start
                q = strided(q_ref, q_row, Q_BLK, d) * (LOG2_E * HEAD_DIM ** -0.5)
                kb = strided(k_ref, k_row, K_BLK, d).astype(BF16)
                v = strided(v_ref, k_row, K_BLK, d)
                v_scr[j, 0] = jnp.where(head0_k, v, 1.0).astype(BF16)
                v_scr[j, 1] = jnp.where(head0_k, 1.0, v).astype(BF16)
                q_rows.append(q_row)
                for h2 in range(HEADS_PER_STEP):
                    sel = head0 if h2 == 0 else jnp.logical_not(head0)
                    qh = jnp.where(sel, q, 0.0).astype(BF16)
                    s = lax.dot_general(qh, kb, (((1,), (1,)), ((), ())), preferred_element_type=F32)
                    s_scr[j * HEADS_PER_STEP + h2] = (
                        s + tab_ref[0, (b * N_VARIANTS + variant) * HEADS_PER_STEP + h2])
            for j in range(ATT_GROUP):
                ms = []
                for h2 in range(HEADS_PER_STEP):
                    s = s_scr[j * HEADS_PER_STEP + h2]
                    m = jnp.max(s, axis=-1, keepdims=True)
                    p_scr[j * HEADS_PER_STEP + h2] = jnp.exp2(s - m).astype(BF16)
                    ms.append(m)
                strided_store(m_scr.at[b], q_rows[j], d, jnp.where(head0, ms[0], ms[1]))
            for j in range(ATT_GROUP):
                outs = [jnp.dot(p_scr[j * HEADS_PER_STEP + h2], v_scr[j, h2], preferred_element_type=F32)
                        for h2 in range(HEADS_PER_STEP)]
                strided_store(acc_scr.at[b], q_rows[j], d, jnp.where(head0, outs[0], outs[1]))
                strided_store(den_scr.at[b], q_rows[j], d, jnp.where(head0, outs[1], outs[0]))

    m_all = m_scr[...]
    m = jnp.max(m_all, axis=0)
    num = jnp.zeros((ATT_TILE, LANES), F32)
    den = jnp.zeros((ATT_TILE, LANES), F32)
    for b in range(len(DIL_RATES)):
        w = jnp.exp2(m_all[b] - m)
        num = num + w * acc_scr[b]
        den = den + w * pltpu.roll(den_scr[b], HEAD_DIM, axis=1)
    o_ref[...] = num / den


def _attention(qkv, tab, n_seq, seq_len):
    n_hp = ATT_HEADS // HEADS_PER_STEP
    n_t = seq_len // ATT_TILE
    n_br = len(DIL_RATES)
    return pl.pallas_call(
        functools.partial(_att_kernel, seq_len=seq_len),
        grid=(n_seq, n_hp, n_t),
        in_specs=[
            pl.BlockSpec((ATT_TILE, LANES), lambda s, h, t: (s * n_t + t, h)),
            pl.BlockSpec((seq_len, LANES), lambda s, h, t: (s, n_hp + h)),
            pl.BlockSpec((seq_len, LANES), lambda s, h, t: (s, 2 * n_hp + h)),
            pl.BlockSpec((1,) + tab.shape[1:], lambda s, h, t: (h, 0, 0, 0)),
        ],
        out_specs=pl.BlockSpec((ATT_TILE, LANES), lambda s, h, t: (s * n_t + t, h)),
        out_shape=jax.ShapeDtypeStruct((n_seq * seq_len, ATT_WIDTH), F32),
        scratch_shapes=[
            pltpu.VMEM((n_br, ATT_TILE, LANES), F32),
            pltpu.VMEM((n_br, ATT_TILE, LANES), F32),
            pltpu.VMEM((n_br, ATT_TILE, LANES), F32),
            pltpu.VMEM((ATT_GROUP * HEADS_PER_STEP, Q_BLK, K_BLK), F32),
            pltpu.VMEM((ATT_GROUP * HEADS_PER_STEP, Q_BLK, K_BLK), BF16),
            pltpu.VMEM((ATT_GROUP, HEADS_PER_STEP, K_BLK, LANES), BF16),
        ],
        compiler_params=pltpu.CompilerParams(
            dimension_semantics=("arbitrary", "arbitrary", "arbitrary"),
            vmem_limit_bytes=VMEM_LIMIT),
        name="dilated_attention",
    )(qkv, qkv, qkv, tab)


def _ssm_weights(a_re, a_im, log_dt, b_re, b_im, c_re, c_im, d_skip):
    hi = lax.Precision.HIGHEST
    G, N, HC = SSM_GROUPS, SSM_STATE, SSM_GROUP_CH
    dt = jnp.exp(log_dt)[..., None]
    mag = jnp.exp(a_re * dt)
    ab_re = mag * jnp.cos(a_im * dt)
    ab_im = mag * jnp.sin(a_im * dt)
    inv = 1.0 / (a_re * a_re + a_im * a_im)
    f_re = ((ab_re - 1.0) * a_re + ab_im * a_im) * inv
    f_im = (ab_im * a_re - (ab_re - 1.0) * a_im) * inv
    bb_re = f_re[..., None] * b_re - f_im[..., None] * b_im
    bb_im = f_re[..., None] * b_im + f_im[..., None] * b_re
    tau = jnp.arange(CHUNK + 1, dtype=F32)
    mag_t = jnp.exp((a_re * dt)[..., None] * tau)
    pr = mag_t * jnp.cos((a_im * dt)[..., None] * tau)
    pi = mag_t * jnp.sin((a_im * dt)[..., None] * tau)

    def a_pow_b(direction, rev):
        order = slice(CHUNK - 1, None, -1) if rev else slice(0, CHUNK)
        pr_c = jnp.repeat(pr[direction][..., order], HC, axis=-1)
        pi_c = jnp.repeat(pi[direction][..., order], HC, axis=-1)
        br = jnp.tile(bb_re[direction], (1, 1, CHUNK))
        bi = jnp.tile(bb_im[direction], (1, 1, CHUNK))
        return pr_c * br - pi_c * bi, pr_c * bi + pi_c * br

    def lag_table(direction, ab):
        return (jnp.einsum('gcn,gnx->gcx', c_re[direction], ab[0], precision=hi)
                - jnp.einsum('gcn,gnx->gcx', c_im[direction], ab[1], precision=hi))

    ab_f = a_pow_b(0, True)
    ab_b = a_pow_b(1, False)
    lag = jnp.stack([lag_table(0, ab_f), lag_table(1, ab_b)], axis=1)
    w_state = jnp.concatenate(
        [ab_f[0], ab_f[1], ab_f[1], ab_f[0], ab_b[0], ab_b[1], ab_b[1], ab_b[0]], axis=1).astype(BF16)
    pr = jnp.moveaxis(pr, -1, 0)
    pi = jnp.moveaxis(pi, -1, 0)

    def state_out(direction, pr_sel, pi_sel):
        cr = c_re[direction][:, None, :, :]
        ci = c_im[direction][:, None, :, :]
        pr_s = pr_sel.transpose(1, 0, 2)[:, :, None, :]
        pi_s = pi_sel.transpose(1, 0, 2)[:, :, None, :]
        from_re = (cr * pr_s - ci * pi_s).reshape(G, CHUNK_COLS, N)
        from_im = (-cr * pi_s - ci * pr_s).reshape(G, CHUNK_COLS, N)
        return jnp.concatenate([from_re, from_im], axis=2)

    c_pow = jnp.concatenate(
        [state_out(0, pr[1:CHUNK + 1, 0], pi[1:CHUNK + 1, 0]),
         state_out(1, pr[CHUNK:0:-1, 1], pi[CHUNK:0:-1, 1])], axis=2).astype(BF16)

    def packed(direction):
        ar, ai = pr[CHUNK, direction], pi[CHUNK, direction]
        return [jnp.concatenate([ar, ar], -1), jnp.concatenate([-ai, ai], -1),
                jnp.concatenate([ai, -ai], -1)]
    a_chunk = jnp.stack(packed(0) + packed(1), axis=1)
    d_col = jnp.tile(d_skip[:, None, :], (1, CHUNK, 1)).reshape(G, CHUNK_COLS, 1)
    return lag, w_state, c_pow, a_chunk, d_col


def _ssm_kernel(utp_ref, uts_ref, lag_ref, ws_ref, cp_ref, a_ref, d_ref, ytp_ref, yts_ref, g_scr, h_scr,
                *, n_chunks, n_p, n_s):
    seqs = [(utp_ref, ytp_ref, s) for s in range(n_p)] + [(uts_ref, yts_ref, s) for s in range(n_s)]
    n_state = 2 * SSM_STATE

    @pl.when(pl.program_id(0) == 0)
    def _():
        g_scr[...] = jnp.zeros(g_scr.shape, F32)

    lag_f, lag_b = lag_ref[0, 0], lag_ref[0, 1]
    col_j = lax.broadcasted_iota(jnp.int32, (SSM_GROUP_CH, CHUNK_COLS), 1) // SSM_GROUP_CH
    blocks = []
    for i in range(CHUNK):
        fwd = pltpu.roll(lag_f, (i + 1) * SSM_GROUP_CH % CHUNK_COLS, axis=1)
        bwd = pltpu.roll(lag_b, i * SSM_GROUP_CH, axis=1)
        blocks.append(jnp.where(col_j <= i, fwd, 0.0) + jnp.where(col_j >= i, bwd, 0.0))
    w_intra = jnp.concatenate(blocks, axis=0).astype(BF16)
    w_state = ws_ref[0]

    for slot, (u_ref, y_ref, s) in enumerate(seqs):
        lanes = slice(s * n_chunks, (s + 1) * n_chunks)
        x_t = u_ref[:, :, lanes].reshape(CHUNK_COLS, n_chunks).astype(BF16)
        r = jnp.dot(w_intra, x_t, preferred_element_type=F32)
        y_ref[:, :, lanes] = r.reshape(CHUNK, SSM_GROUP_CH, n_chunks)
        r = jnp.dot(w_state, x_t, preferred_element_type=F32)
        for k in range(4):
            g_scr.at[k][pl.ds(slot, n_chunks, stride=SEQ_PAD), :] = r[k * n_state:(k + 1) * n_state].T

    a = a_ref[0]
    shape = (SEQ_PAD, LANES)
    pf, qf, q2f, pb, qb, q2b = [jnp.broadcast_to(a[k:k + 1], shape) for k in range(6)]

    def step(c, carry):
        hf1, hf2, hb1, hb2 = carry
        rf = pl.ds(pl.multiple_of(c * SEQ_PAD, SEQ_PAD), SEQ_PAD)
        rb = pl.ds(pl.multiple_of((n_chunks - 1 - c) * SEQ_PAD, SEQ_PAD), SEQ_PAD)
        h_scr[0, rf, :] = hf1
        h_scr[1, rb, :] = hb1
        return (pf * hf1 + qf * hf2 + g_scr[0, rf, :], pf * hf2 + q2f * hf1 + g_scr[1, rf, :],
                pb * hb1 + qb * hb2 + g_scr[2, rb, :], pb * hb2 + q2b * hb1 + g_scr[3, rb, :])

    zero = jnp.zeros(shape, F32)
    lax.fori_loop(0, n_chunks, step, (zero, zero, zero, zero))

    cp = cp_ref[0]
    d_col = d_ref[0]
    for slot, (u_ref, y_ref, s) in enumerate(seqs):
        lanes = slice(s * n_chunks, (s + 1) * n_chunks)
        h_t = jnp.concatenate(
            [h_scr.at[k][pl.ds(slot, n_chunks, stride=SEQ_PAD), :].T for k in range(2)], axis=0)
        x_t = u_ref[:, :, lanes].reshape(CHUNK_COLS, n_chunks)
        y = (y_ref[:, :, lanes].reshape(CHUNK_COLS, n_chunks)
             + jnp.dot(cp, h_t.astype(BF16), preferred_element_type=F32) + d_col * x_t)
        y_ref[:, :, lanes] = y.reshape(CHUNK, SSM_GROUP_CH, n_chunks)


def _ssm(ut_p, ut_s, lag, w_state, c_pow, a_chunk, d_col, n_chunks):
    G = SSM_GROUPS
    n_p = ut_p.shape[2] // n_chunks
    n_s = ut_s.shape[2] // n_chunks
    n_rows = n_chunks * SEQ_PAD
    act = lambda arr: pl.BlockSpec((CHUNK, SSM_GROUP_CH, arr.shape[2]), lambda g: (0, g, 0))
    per_group = lambda arr: pl.BlockSpec((1,) + arr.shape[1:], lambda g: (g,) + (0,) * (arr.ndim - 1))
    return pl.pallas_call(
        functools.partial(_ssm_kernel, n_chunks=n_chunks, n_p=n_p, n_s=n_s),
        grid=(G,),
        in_specs=[act(ut_p), act(ut_s), per_group(lag), per_group(w_state), per_group(c_pow),
                  per_group(a_chunk), per_group(d_col)],
        out_specs=[act(ut_p), act(ut_s)],
        out_shape=[jax.ShapeDtypeStruct(ut_p.shape, F32), jax.ShapeDtypeStruct(ut_s.shape, F32)],
        scratch_shapes=[
            pltpu.VMEM((4, n_rows, LANES), F32),
            pltpu.VMEM((2, n_rows, LANES), F32),
        ],
        compiler_params=pltpu.CompilerParams(
            dimension_semantics=("arbitrary",), vmem_limit_bytes=VMEM_LIMIT),
        name="ssm_chunked",
    )(ut_p, ut_s, lag, w_state, c_pow, a_chunk, d_col)


def _post_kernel(x_ref, att_ref, yt_ref, p_ref, wglu_ref, bglu_ref, gatt_ref, gssm_ref, wout_ref,
                 gmlp_ref, w1_ref, w2_ref, gple_ref, wgate_ref, wproj_ref, gfin_ref, o_ref, ys_scr):
    k = pl.program_id(1)

    @pl.when(k == 0)
    def _():
        for l in range(SSM_WIDTH // LANES):
            for i in range(CHUNK):
                ys_scr.at[l][pl.ds(i, LANES, stride=CHUNK), :] = yt_ref[i, l * LANES:(l + 1) * LANES, :].T

    rows = pl.ds(pl.multiple_of(k * POST_TILE, POST_TILE), POST_TILE)
    ys = jnp.concatenate([ys_scr[l, rows, :] for l in range(SSM_WIDTH // LANES)], axis=1)
    g = _gelu_tanh(ys)
    ssm = g * _sigmoid(_bdot(g, wglu_ref[...]) + bglu_ref[...])
    att_n = _rms(att_ref[...], gatt_ref[...])
    ssm_n = _rms(ssm, gssm_ref[...])
    h = x_ref[...] + (_bdot(att_n, wout_ref[:ATT_WIDTH, :]) + _bdot(ssm_n, wout_ref[ATT_WIDTH:, :]))
    f = _rms(h, gmlp_ref[...]).astype(BF16)
    acc = jnp.zeros_like(h)
    for kb in range(D_FF // FF_BLK):
        cols = slice(kb * FF_BLK, (kb + 1) * FF_BLK)
        t = jnp.dot(f, w1_ref[:, cols], preferred_element_type=F32)
        t = jnp.square(jnp.maximum(t, 0.0))
        acc = acc + _bdot(t, w2_ref[cols, :])
    h = h + acc
    e = _rms(h, gple_ref[...])
    h = h + _sigmoid(_bdot(e, wgate_ref[...])) * _bdot(p_ref[...], wproj_ref[...])
    o_ref[...] = _rms(h, gfin_ref[...])


def _post(x, att, yt, p, wts):
    t = x.shape[0]
    n_in = SUPER_TILE // POST_TILE
    tile = lambda width: pl.BlockSpec((POST_TILE, width), lambda i, k: (i * n_in + k, 0))

    def resident(arr):
        return pl.BlockSpec(arr.shape, lambda i, k: (0,) * arr.ndim, pipeline_mode=pl.Buffered(1))

    return pl.pallas_call(
        _post_kernel,
        grid=(t // SUPER_TILE, n_in),
        in_specs=[tile(D_MODEL), tile(ATT_WIDTH),
                  pl.BlockSpec((CHUNK, SSM_WIDTH, LANES), lambda i, k: (0, 0, i)), tile(PLE_DIM)]
                 + [resident(w) for w in wts],
        out_specs=tile(D_MODEL),
        out_shape=jax.ShapeDtypeStruct((t, D_MODEL), F32),
        scratch_shapes=[pltpu.VMEM((SSM_WIDTH // LANES, SUPER_TILE, LANES), F32)],
        compiler_params=pltpu.CompilerParams(
            dimension_semantics=("arbitrary", "arbitrary"), vmem_limit_bytes=VMEM_LIMIT),
        name="post_mixers",
    )(x, att, yt, p, *wts)


def kernel(x_prompt, x_sample, p_prompt, p_sample, rel_bias, g_mix, w_in, ssm_a_re, ssm_a_im, ssm_log_dt, ssm_b_re, ssm_b_im, ssm_c_re, ssm_c_im, ssm_d, w_glu, b_glu, g_att_out, g_ssm_out, w_out, g_mlp, w_mlp1, w_mlp2, g_ple, w_ple_gate, w_ple_proj, g_final):
    assert g_mix.shape[0] == 1, "single-layer trunk"
    seq_len = x_prompt.shape[1]
    assert x_sample.shape[1] == seq_len and seq_len % ATT_TILE == 0
    n_p, n_s = x_prompt.shape[0], x_sample.shape[0]
    assert n_p + n_s <= SEQ_PAD
    n_chunks = seq_len // CHUNK

    w_in_bf = w_in[0].astype(BF16)
    tab = _bias_tables(rel_bias)
    ssm_wts = _ssm_weights(
        ssm_a_re[0], ssm_a_im[0], ssm_log_dt[0], ssm_b_re[0], ssm_b_im[0],
        ssm_c_re[0], ssm_c_im[0], ssm_d[0])
    row = lambda v, n: v.reshape(1, n)
    wts = (w_glu[0].astype(BF16), row(b_glu[0], SSM_WIDTH), row(g_att_out[0], ATT_WIDTH),
           row(g_ssm_out[0], SSM_WIDTH), w_out[0].astype(BF16), row(g_mlp[0], D_MODEL),
           w_mlp1[0].astype(BF16), w_mlp2[0].astype(BF16), row(g_ple[0], D_MODEL),
           w_ple_gate[0].astype(BF16), w_ple_proj[0].astype(BF16), row(g_final, D_MODEL))

    groups = []
    for x3, p4 in ((x_prompt, p_prompt), (x_sample, p_sample)):
        n = x3.shape[0]
        x = x3.reshape(n * seq_len, D_MODEL)
        qkv, ut = _inproj(x, g_mix[0], w_in_bf)
        att = _attention(qkv, tab, n, seq_len)
        groups.append((x, p4[0].reshape(n * seq_len, PLE_DIM), att, ut))
    yts = _ssm(groups[0][3], groups[1][3], *ssm_wts, n_chunks)
    outs = [_post(x, att, yt, p, wts).reshape(-1, seq_len, D_MODEL)
            for (x, p, att, _), yt in zip(groups, yts)]
    return outs[0], outs[1]
```

```python
import functools
import math

import jax
import jax.numpy as jnp
import numpy as np
from jax import lax
from jax.experimental import pallas as pl
from jax.experimental.pallas import tpu as pltpu

F32 = jnp.float32
BF16 = jnp.bfloat16

D_MODEL = 1024
ATT_HEADS = 8
HEAD_DIM = 64
ATT_WIDTH = ATT_HEADS * HEAD_DIM
SSM_WIDTH = D_MODEL - ATT_WIDTH
SSM_GROUP_CH = 16
SSM_GROUPS = SSM_WIDTH // SSM_GROUP_CH
SSM_STATE = 64
IN_COLS = 3 * ATT_WIDTH + SSM_WIDTH
D_FF = 4 * D_MODEL
PLE_DIM = 256
NUM_BUCKETS = 32
REL_MAX_DISTANCE = 1024
DIL_WINDOWS = (128, 512, 2048)
DIL_RATES = (1, 4, 16)
RMS_EPS = 1e-6
NEG_INF = -1e30
LOG2_E = math.log2(math.e)

LANES = 128
SUBLANES = 8
VMEM_LIMIT = 56 * 1024 * 1024

RADIUS = 64
Q_BLK = 128
K_BLK = Q_BLK + 2 * RADIUS
ATT_TILE = 2048
PHASES = 4
HEADS_PER_STEP = LANES // HEAD_DIM
ATT_GROUP = 16
N_VARIANTS = 3

CHUNK = 16
CHUNK_COLS = CHUNK * SSM_GROUP_CH
SEQ_PAD = SUBLANES

SUPER_TILE = CHUNK * LANES
IN_TILE = 512
POST_TILE = 512
FF_BLK = 1024


def _rms(x, g):
    return x * lax.rsqrt(jnp.mean(x * x, axis=-1, keepdims=True) + RMS_EPS) * g


def _sigmoid(x):
    return 1.0 / (1.0 + jnp.exp(-x))


def _gelu_tanh(x):
    c = math.sqrt(2.0 / math.pi)
    return 0.5 * x * (1.0 + jnp.tanh(c * (x + 0.044715 * (x * x * x))))


def _bdot(a, b):
    return jnp.dot(a.astype(BF16), b.astype(BF16), preferred_element_type=F32)


def _inproj_kernel(x_ref, g_ref, w_ref, qkv_ref, ut_ref, u_scr):
    k = pl.program_id(1)
    a = _rms(x_ref[...], g_ref[...])
    z = _bdot(a, w_ref[...])
    qkv_ref[...] = z[:, :3 * ATT_WIDTH]
    rows = pl.ds(pl.multiple_of(k * IN_TILE, IN_TILE), IN_TILE)
    for l in range(SSM_WIDTH // LANES):
        u_scr[l, rows, :] = z[:, 3 * ATT_WIDTH + l * LANES:3 * ATT_WIDTH + (l + 1) * LANES]

    @pl.when(k == SUPER_TILE // IN_TILE - 1)
    def _():
        for l in range(SSM_WIDTH // LANES):
            for j in range(CHUNK):
                ut_ref[j, l * LANES:(l + 1) * LANES, :] = u_scr.at[l][pl.ds(j, LANES, stride=CHUNK), :].T


def _inproj(x, g_mix, w_in_bf):
    t = x.shape[0]
    n_in = SUPER_TILE // IN_TILE
    return pl.pallas_call(
        _inproj_kernel,
        grid=(t // SUPER_TILE, n_in),
        in_specs=[
            pl.BlockSpec((IN_TILE, D_MODEL), lambda i, k: (i * n_in + k, 0)),
            pl.BlockSpec((1, D_MODEL), lambda i, k: (0, 0)),
            pl.BlockSpec((D_MODEL, IN_COLS), lambda i, k: (0, 0)),
        ],
        out_specs=[
            pl.BlockSpec((IN_TILE, 3 * ATT_WIDTH), lambda i, k: (i * n_in + k, 0)),
            pl.BlockSpec((CHUNK, SSM_WIDTH, LANES), lambda i, k: (0, 0, i)),
        ],
        out_shape=[
            jax.ShapeDtypeStruct((t, 3 * ATT_WIDTH), F32),
            jax.ShapeDtypeStruct((CHUNK, SSM_WIDTH, t // CHUNK), F32),
        ],
        scratch_shapes=[pltpu.VMEM((SSM_WIDTH // LANES, SUPER_TILE, LANES), F32)],
        compiler_params=pltpu.CompilerParams(
            dimension_semantics=("arbitrary", "arbitrary"), vmem_limit_bytes=VMEM_LIMIT),
        name="inproj",
    )(x, g_mix.reshape(1, D_MODEL), w_in_bf)


def _t5_bucket_np(rel):
    half = NUM_BUCKETS // 2
    n = -rel
    ret = np.where(n < 0, half, 0)
    n = np.abs(n)
    max_exact = half // 2
    nf = np.maximum(n, 1).astype(np.float64)
    large = max_exact + (np.log(nf / max_exact) / math.log(REL_MAX_DISTANCE / max_exact)
                         * (half - max_exact)).astype(np.int64)
    large = np.minimum(large, half - 1)
    return ret + np.where(n < max_exact, n, large)


def _bucket_tables():
    out = np.zeros((len(DIL_RATES), N_VARIANTS, Q_BLK, K_BLK), np.int32)
    for b, d in enumerate(DIL_RATES):
        qi = np.arange(Q_BLK)
        ci = np.arange(K_BLK)
        if d == 1:
            qi = (qi % (Q_BLK // PHASES)) * PHASES + qi // (Q_BLK // PHASES)
            ci = (ci % (K_BLK // PHASES)) * PHASES + ci // (K_BLK // PHASES)
        for v, shift in enumerate((0, -RADIUS, -2 * RADIUS)):
            off = ci[None, :] - qi[:, None] + shift
            bk = _t5_bucket_np(off * d)
            out[b, v] = np.where(np.abs(off) <= RADIUS, bk, NUM_BUCKETS)
    return out.reshape(len(DIL_RATES) * N_VARIANTS, Q_BLK, K_BLK)


def _bias_kernel(rel_ref, bk_ref, tab_ref):
    hp = pl.program_id(0)
    n_bv = bk_ref.shape[0]
    for bv in range(n_bv):
        bk = bk_ref[bv]
        for h2 in range(HEADS_PER_STEP):
            acc = jnp.full(bk.shape, NEG_INF, F32)
            for b in range(NUM_BUCKETS):
                acc = jnp.where(bk == b, rel_ref[b, hp * HEADS_PER_STEP + h2] * LOG2_E, acc)
            tab_ref[0, bv * HEADS_PER_STEP + h2] = acc


def _bias_tables(rel_bias):
    bk = jnp.asarray(_bucket_tables())
    n_bv = bk.shape[0]
    n_hp = ATT_HEADS // HEADS_PER_STEP
    return pl.pallas_call(
        _bias_kernel,
        grid=(n_hp,),
        in_specs=[
            pl.BlockSpec(memory_space=pltpu.SMEM),
            pl.BlockSpec((n_bv, Q_BLK, K_BLK), lambda h: (0, 0, 0)),
        ],
        out_specs=pl.BlockSpec((1, n_bv * HEADS_PER_STEP, Q_BLK, K_BLK), lambda h: (h, 0, 0, 0)),
        out_shape=jax.ShapeDtypeStruct((n_hp, n_bv * HEADS_PER_STEP, Q_BLK, K_BLK), F32),
        compiler_params=pltpu.CompilerParams(dimension_semantics=("arbitrary",)),
        name="bias_tables",
    )(rel_bias, bk)


def _att_kernel(q_ref, k_ref, v_ref, tab_ref, o_ref, k4_scr, va_scr, vb_scr, qa_scr, qb_scr,
                acc_scr, m_scr, den_scr, s_scr, p_scr, v_scr, *, seq_len):
    t = pl.program_id(2)
    n4 = seq_len // PHASES
    t4 = ATT_TILE // PHASES
    n_sub = ATT_TILE // Q_BLK
    stage = 256

    def head0_mask(rows):
        return lax.broadcasted_iota(jnp.int32, (rows, LANES), 1) < HEAD_DIM

    head0 = head0_mask(Q_BLK)
    head0_s = head0_mask(stage)

    @pl.when(t == 0)
    def _():
        def body(c, carry):
            for r in range(PHASES):
                src = pl.ds(r + PHASES * stage * c, stage, stride=PHASES)
                dst = pl.ds(pl.multiple_of(stage * c, stage), stage)
                k4_scr[r, dst, :] = k_ref[src, :]
                v = v_ref[src, :]
                va_scr[r, dst, :] = jnp.where(head0_s, v, 1.0)
                vb_scr[r, dst, :] = jnp.where(head0_s, 1.0, v)
            return carry
        lax.fori_loop(0, n4 // stage, body, 0)

    for r in range(PHASES):
        for c in range(t4 // stage):
            q = q_ref[pl.ds(r + PHASES * stage * c, stage, stride=PHASES), :] * (LOG2_E * HEAD_DIM ** -0.5)
            qa_scr[r, c * stage:(c + 1) * stage, :] = jnp.where(head0_s, q, 0.0)
            qb_scr[r, c * stage:(c + 1) * stage, :] = jnp.where(head0_s, 0.0, q)

    def pieces(scr, start, n):
        return jnp.concatenate([scr[r, pl.ds(start, n), :] for r in range(PHASES)], axis=0)

    def sub_block(b, idx):
        d = DIL_RATES[b]
        n_m = seq_len // d
        if d == 1:
            m0 = t * ATT_TILE + Q_BLK * idx
        elif d == PHASES:
            r, blk = idx % PHASES, idx // PHASES
            m0 = t * t4 + Q_BLK * blk
        else:
            r, r16 = idx % PHASES, idx // PHASES
            m0 = t * (ATT_TILE // d)
        k_start = jnp.clip(m0 - RADIUS, 0, n_m - K_BLK)
        variant = jnp.where(m0 < RADIUS, 0, jnp.where(m0 > n_m - Q_BLK - RADIUS, 2, 1))
        if d == 1:
            q0 = (Q_BLK // PHASES) * idx
            k0 = pl.multiple_of(k_start // PHASES, SUBLANES)
            load_q = lambda scr: pieces(scr, q0, Q_BLK // PHASES)
            load_k = lambda scr: pieces(scr, k0, K_BLK // PHASES)

            def store(scr, val):
                n = Q_BLK // PHASES
                for rr in range(PHASES):
                    scr[b, rr, q0:q0 + n, :] = val[rr * n:(rr + 1) * n]
        elif d == PHASES:
            load_q = lambda scr: scr[r, blk * Q_BLK:(blk + 1) * Q_BLK, :]
            load_k = lambda scr: scr[r, pl.ds(pl.multiple_of(k_start, SUBLANES), K_BLK), :]

            def store(scr, val):
                scr[b, r, blk * Q_BLK:(blk + 1) * Q_BLK, :] = val
        else:
            e = d // PHASES
            load_q = lambda scr: scr.at[r][pl.ds(r16, Q_BLK, stride=e), :]
            load_k = lambda scr: scr.at[r][pl.ds(r16 + e * k_start, K_BLK, stride=e), :]

            def store(scr, val):
                scr.at[b, r][pl.ds(r16, Q_BLK, stride=e), :] = val
        return load_q, load_k, store, variant

    for b in range(len(DIL_RATES)):
        for g0 in range(0, n_sub, ATT_GROUP):
            stores = []
            for j in range(ATT_GROUP):
                load_q, load_k, store, variant = sub_block(b, g0 + j)
                stores.append(store)
                kb = load_k(k4_scr).astype(BF16)
                v_scr[j, 0] = load_k(va_scr).astype(BF16)
                v_scr[j, 1] = load_k(vb_scr).astype(BF16)
                for h2, q_scr in enumerate((qa_scr, qb_scr)):
                    s = lax.dot_general(load_q(q_scr).astype(BF16), kb, (((1,), (1,)), ((), ())),
                                        preferred_element_type=F32)
                    s_scr[j * HEADS_PER_STEP + h2] = (
                        s + tab_ref[0, (b * N_VARIANTS + variant) * HEADS_PER_STEP + h2])
            for j in range(ATT_GROUP):
                ms = []
                for h2 in range(HEADS_PER_STEP):
                    s = s_scr[j * HEADS_PER_STEP + h2]
                    m = jnp.max(s, axis=-1, keepdims=True)
                    p_scr[j * HEADS_PER_STEP + h2] = jnp.exp2(s - m).astype(BF16)
                    ms.append(m)
                stores[j](m_scr, jnp.where(head0, ms[0], ms[1]))
            for j in range(ATT_GROUP):
                outs = [jnp.dot(p_scr[j * HEADS_PER_STEP + h2], v_scr[j, h2], preferred_element_type=F32)
                        for h2 in range(HEADS_PER_STEP)]
                stores[j](acc_scr, jnp.where(head0, outs[0], outs[1]))
                stores[j](den_scr, jnp.where(head0, outs[1], outs[0]))

    for r in range(PHASES):
        m_all = m_scr[:, r]
        m = jnp.max(m_all, axis=0)
        num = jnp.zeros((t4, LANES), F32)
        den = jnp.zeros((t4, LANES), F32)
        for b in range(len(DIL_RATES)):
            w = jnp.exp2(m_all[b] - m)
            num = num + w * acc_scr[b, r]
            den = den + w * pltpu.roll(den_scr[b, r], HEAD_DIM, axis=1)
        o_ref[pl.ds(r, t4, stride=PHASES), :] = num / den


def _attention(qkv, tab, n_seq, seq_len):
    n_hp = ATT_HEADS // HEADS_PER_STEP
    n_t = seq_len // ATT_TILE
    n_br = len(DIL_RATES)
    return pl.pallas_call(
        functools.partial(_att_kernel, seq_len=seq_len),
        grid=(n_seq, n_hp, n_t),
        in_specs=[
            pl.BlockSpec((ATT_TILE, LANES), lambda s, h, t: (s * n_t + t, h)),
            pl.BlockSpec((seq_len, LANES), lambda s, h, t: (s, n_hp + h), pipeline_mode=pl.Buffered(1)),
            pl.BlockSpec((seq_len, LANES), lambda s, h, t: (s, 2 * n_hp + h), pipeline_mode=pl.Buffered(1)),
            pl.BlockSpec((1,) + tab.shape[1:], lambda s, h, t: (h, 0, 0, 0)),
        ],
        out_specs=pl.BlockSpec((ATT_TILE, LANES), lambda s, h, t: (s * n_t + t, h)),
        out_shape=jax.ShapeDtypeStruct((n_seq * seq_len, ATT_WIDTH), F32),
        scratch_shapes=[
            pltpu.VMEM((PHASES, seq_len // PHASES, LANES), F32),
            pltpu.VMEM((PHASES, seq_len // PHASES, LANES), F32),
            pltpu.VMEM((PHASES, seq_len // PHASES, LANES), F32),
            pltpu.VMEM((PHASES, ATT_TILE // PHASES, LANES), F32),
            pltpu.VMEM((PHASES, ATT_TILE // PHASES, LANES), F32),
            pltpu.VMEM((n_br, PHASES, ATT_TILE // PHASES, LANES), F32),
            pltpu.VMEM((n_br, PHASES, ATT_TILE // PHASES, LANES), F32),
            pltpu.VMEM((n_br, PHASES, ATT_TILE // PHASES, LANES), F32),
            pltpu.VMEM((ATT_GROUP * HEADS_PER_STEP, Q_BLK, K_BLK), F32),
            pltpu.VMEM((ATT_GROUP * HEADS_PER_STEP, Q_BLK, K_BLK), BF16),
            pltpu.VMEM((ATT_GROUP, HEADS_PER_STEP, K_BLK, LANES), BF16),
        ],
        compiler_params=pltpu.CompilerParams(
            dimension_semantics=("arbitrary", "arbitrary", "arbitrary"),
            vmem_limit_bytes=VMEM_LIMIT),
        name="dilated_attention",
    )(qkv, qkv, qkv, tab)


def _ssm_weights(a_re, a_im, log_dt, b_re, b_im, c_re, c_im, d_skip):
    hi = lax.Precision.HIGHEST
    G, N, HC = SSM_GROUPS, SSM_STATE, SSM_GROUP_CH
    dt = jnp.exp(log_dt)[..., None]
    mag = jnp.exp(a_re * dt)
    ab_re = mag * jnp.cos(a_im * dt)
    ab_im = mag * jnp.sin(a_im * dt)
    inv = 1.0 / (a_re * a_re + a_im * a_im)
    f_re = ((ab_re - 1.0) * a_re + ab_im * a_im) * inv
    f_im = (ab_im * a_re - (ab_re - 1.0) * a_im) * inv
    bb_re = f_re[..., None] * b_re - f_im[..., None] * b_im
    bb_im = f_re[..., None] * b_im + f_im[..., None] * b_re
    tau = jnp.arange(CHUNK + 1, dtype=F32)
    mag_t = jnp.exp((a_re * dt)[..., None] * tau)
    pr = mag_t * jnp.cos((a_im * dt)[..., None] * tau)
    pi = mag_t * jnp.sin((a_im * dt)[..., None] * tau)

    def a_pow_b(direction, rev):
        order = slice(CHUNK - 1, None, -1) if rev else slice(0, CHUNK)
        pr_c = jnp.repeat(pr[direction][..., order], HC, axis=-1)
        pi_c = jnp.repeat(pi[direction][..., order], HC, axis=-1)
        br = jnp.tile(bb_re[direction], (1, 1, CHUNK))
        bi = jnp.tile(bb_im[direction], (1, 1, CHUNK))
        return pr_c * br - pi_c * bi, pr_c * bi + pi_c * br

    def lag_table(direction, ab):
        return (jnp.einsum('gcn,gnx->gcx', c_re[direction], ab[0], precision=hi)
                - jnp.einsum('gcn,gnx->gcx', c_im[direction], ab[1], precision=hi))

    ab_f = a_pow_b(0, True)
    ab_b = a_pow_b(1, False)
    lag = jnp.stack([lag_table(0, ab_f), lag_table(1, ab_b)], axis=1)
    w_state = jnp.concatenate(
        [ab_f[0], ab_f[1], ab_f[1], ab_f[0], ab_b[0], ab_b[1], ab_b[1], ab_b[0]], axis=1).astype(BF16)
    pr = jnp.moveaxis(pr, -1, 0)
    pi = jnp.moveaxis(pi, -1, 0)

    def state_out(direction, pr_sel, pi_sel):
        cr = c_re[direction][:, None, :, :]
        ci = c_im[direction][:, None, :, :]
        pr_s = pr_sel.transpose(1, 0, 2)[:, :, None, :]
        pi_s = pi_sel.transpose(1, 0, 2)[:, :, None, :]
        from_re = (cr * pr_s - ci * pi_s).reshape(G, CHUNK_COLS, N)
        from_im = (-cr * pi_s - ci * pr_s).reshape(G, CHUNK_COLS, N)
        return jnp.concatenate([from_re, from_im], axis=2)

    c_pow = jnp.concatenate(
        [state_out(0, pr[1:CHUNK + 1, 0], pi[1:CHUNK + 1, 0]),
         state_out(1, pr[CHUNK:0:-1, 1], pi[CHUNK:0:-1, 1])], axis=2).astype(BF16)

    def packed(direction):
        ar, ai = pr[CHUNK, direction], pi[CHUNK, direction]
        return [jnp.concatenate([ar, ar], -1), jnp.concatenate([-ai, ai], -1),
                jnp.concatenate([ai, -ai], -1)]
    a_chunk = jnp.stack(packed(0) + packed(1), axis=1)
    d_col = jnp.tile(d_skip[:, None, :], (1, CHUNK, 1)).reshape(G, CHUNK_COLS, 1)
    return lag, w_state, c_pow, a_chunk, d_col


def _ssm_kernel(utp_ref, uts_ref, lag_ref, ws_ref, cp_ref, a_ref, d_ref, ytp_ref, yts_ref, g_scr, h_scr,
                *, n_chunks, n_p, n_s):
    seqs = [(utp_ref, ytp_ref, s) for s in range(n_p)] + [(uts_ref, yts_ref, s) for s in range(n_s)]
    n_state = 2 * SSM_STATE

    @pl.when(pl.program_id(0) == 0)
    def _():
        g_scr[...] = jnp.zeros(g_scr.shape, F32)

    lag_f, lag_b = lag_ref[0, 0], lag_ref[0, 1]
    col_j = lax.broadcasted_iota(jnp.int32, (SSM_GROUP_CH, CHUNK_COLS), 1) // SSM_GROUP_CH
    blocks = []
    for i in range(CHUNK):
        fwd = pltpu.roll(lag_f, (i + 1) * SSM_GROUP_CH % CHUNK_COLS, axis=1)
        bwd = pltpu.roll(lag_b, i * SSM_GROUP_CH, axis=1)
        blocks.append(jnp.where(col_j <= i, fwd, 0.0) + jnp.where(col_j >= i, bwd, 0.0))
    w_intra = jnp.concatenate(blocks, axis=0).astype(BF16)
    w_state = ws_ref[0]

    for slot, (u_ref, y_ref, s) in enumerate(seqs):
        lanes = slice(s * n_chunks, (s + 1) * n_chunks)
        x_t = u_ref[:, :, lanes].reshape(CHUNK_COLS, n_chunks).astype(BF16)
        r = jnp.dot(w_intra, x_t, preferred_element_type=F32)
        y_ref[:, :, lanes] = r.reshape(CHUNK, SSM_GROUP_CH, n_chunks)
        r = jnp.dot(w_state, x_t, preferred_element_type=F32)
        for k in range(4):
            g_scr.at[k][pl.ds(slot, n_chunks, stride=SEQ_PAD), :] = r[k * n_state:(k + 1) * n_state].T

    a = a_ref[0]
    shape = (SEQ_PAD, LANES)
    pf, qf, q2f, pb, qb, q2b = [jnp.broadcast_to(a[k:k + 1], shape) for k in range(6)]

    def step(c, carry):
        hf1, hf2, hb1, hb2 = carry
        rf = pl.ds(pl.multiple_of(c * SEQ_PAD, SEQ_PAD), SEQ_PAD)
        rb = pl.ds(pl.multiple_of((n_chunks - 1 - c) * SEQ_PAD, SEQ_PAD), SEQ_PAD)
        h_scr[0, rf, :] = hf1
        h_scr[1, rb, :] = hb1
        return (pf * hf1 + qf * hf2 + g_scr[0, rf, :], pf * hf2 + q2f * hf1 + g_scr[1, rf, :],
                pb * hb1 + qb * hb2 + g_scr[2, rb, :], pb * hb2 + q2b * hb1 + g_scr[3, rb, :])

    zero = jnp.zeros(shape, F32)
    lax.fori_loop(0, n_chunks, step, (zero, zero, zero, zero))

    cp = cp_ref[0]
    d_col = d_ref[0]
    for slot, (u_ref, y_ref, s) in enumerate(seqs):
        lanes = slice(s * n_chunks, (s + 1) * n_chunks)
        h_t = jnp.concatenate(
            [h_scr.at[k][pl.ds(slot, n_chunks, stride=SEQ_PAD), :].T for k in range(2)], axis=0)
        x_t = u_ref[:, :, lanes].reshape(CHUNK_COLS, n_chunks)
        y = (y_ref[:, :, lanes].reshape(CHUNK_COLS, n_chunks)
             + jnp.dot(cp, h_t.astype(BF16), preferred_element_type=F32) + d_col * x_t)
        y_ref[:, :, lanes] = y.reshape(CHUNK, SSM_GROUP_CH, n_chunks)


def _ssm(ut_p, ut_s, lag, w_state, c_pow, a_chunk, d_col, n_chunks):
    G = SSM_GROUPS
    n_p = ut_p.shape[2] // n_chunks
    n_s = ut_s.shape[2] // n_chunks
    n_rows = n_chunks * SEQ_PAD
    act = lambda arr: pl.BlockSpec((CHUNK, SSM_GROUP_CH, arr.shape[2]), lambda g: (0, g, 0))
    per_group = lambda arr: pl.BlockSpec((1,) + arr.shape[1:], lambda g: (g,) + (0,) * (arr.ndim - 1))
    return pl.pallas_call(
        functools.partial(_ssm_kernel, n_chunks=n_chunks, n_p=n_p, n_s=n_s),
        grid=(G,),
        in_specs=[act(ut_p), act(ut_s), per_group(lag), per_group(w_state), per_group(c_pow),
                  per_group(a_chunk), per_group(d_col)],
        out_specs=[act(ut_p), act(ut_s)],
        out_shape=[jax.ShapeDtypeStruct(ut_p.shape, F32), jax.ShapeDtypeStruct(ut_s.shape, F32)],
        scratch_shapes=[
            pltpu.VMEM((4, n_rows, LANES), F32),
            pltpu.VMEM((2, n_rows, LANES), F32),
        ],
        compiler_params=pltpu.CompilerParams(
            dimension_semantics=("arbitrary",), vmem_limit_bytes=VMEM_LIMIT),
        name="ssm_chunked",
    )(ut_p, ut_s, lag, w_state, c_pow, a_chunk, d_col)


def _post_kernel(x_ref, att_ref, yt_ref, p_ref, wglu_ref, bglu_ref, gatt_ref, gssm_ref, wout_ref,
                 gmlp_ref, w1_ref, w2_ref, gple_ref, wgate_ref, wproj_ref, gfin_ref, o_ref, ys_scr):
    k = pl.program_id(1)

    @pl.when(k == 0)
    def _():
        for l in range(SSM_WIDTH // LANES):
            for i in range(CHUNK):
                ys_scr.at[l][pl.ds(i, LANES, stride=CHUNK), :] = yt_ref[i, l * LANES:(l + 1) * LANES, :].T

    rows = pl.ds(pl.multiple_of(k * POST_TILE, POST_TILE), POST_TILE)
    ys = jnp.concatenate([ys_scr[l, rows, :] for l in range(SSM_WIDTH // LANES)], axis=1)
    g = _gelu_tanh(ys)
    ssm = g * _sigmoid(_bdot(g, wglu_ref[...]) + bglu_ref[...])
    att_n = _rms(att_ref[...], gatt_ref[...])
    ssm_n = _rms(ssm, gssm_ref[...])
    h = x_ref[...] + (_bdot(att_n, wout_ref[:ATT_WIDTH, :]) + _bdot(ssm_n, wout_ref[ATT_WIDTH:, :]))
    f = _rms(h, gmlp_ref[...]).astype(BF16)
    acc = jnp.zeros_like(h)
    for kb in range(D_FF // FF_BLK):
        cols = slice(kb * FF_BLK, (kb + 1) * FF_BLK)
        t = jnp.dot(f, w1_ref[:, cols], preferred_element_type=F32)
        t = jnp.square(jnp.maximum(t, 0.0))
        acc = acc + _bdot(t, w2_ref[cols, :])
    h = h + acc
    e = _rms(h, gple_ref[...])
    h = h + _sigmoid(_bdot(e, wgate_ref[...])) * _bdot(p_ref[...], wproj_ref[...])
    o_ref[...] = _rms(h, gfin_ref[...])


def _post(x, att, yt, p, wts):
    t = x.shape[0]
    n_in = SUPER_TILE // POST_TILE
    tile = lambda width: pl.BlockSpec((POST_TILE, width), lambda i, k: (i * n_in + k, 0))

    def resident(arr):
        return pl.BlockSpec(arr.shape, lambda i, k: (0,) * arr.ndim, pipeline_mode=pl.Buffered(1))

    return pl.pallas_call(
        _post_kernel,
        grid=(t // SUPER_TILE, n_in),
        in_specs=[tile(D_MODEL), tile(ATT_WIDTH),
                  pl.BlockSpec((CHUNK, SSM_WIDTH, LANES), lambda i, k: (0, 0, i)), tile(PLE_DIM)]
                 + [resident(w) for w in wts],
        out_specs=tile(D_MODEL),
        out_shape=jax.ShapeDtypeStruct((t, D_MODEL), F32),
        scratch_shapes=[pltpu.VMEM((SSM_WIDTH // LANES, SUPER_TILE, LANES), F32)],
        compiler_params=pltpu.CompilerParams(
            dimension_semantics=("arbitrary", "arbitrary"), vmem_limit_bytes=VMEM_LIMIT),
        name="post_mixers",
    )(x, att, yt, p, *wts)


def kernel(x_prompt, x_sample, p_prompt, p_sample, rel_bias, g_mix, w_in, ssm_a_re, ssm_a_im, ssm_log_dt, ssm_b_re, ssm_b_im, ssm_c_re, ssm_c_im, ssm_d, w_glu, b_glu, g_att_out, g_ssm_out, w_out, g_mlp, w_mlp1, w_mlp2, g_ple, w_ple_gate, w_ple_proj, g_final):
    assert g_mix.shape[0] == 1, "single-layer trunk"
    seq_len = x_prompt.shape[1]
    assert x_sample.shape[1] == seq_len and seq_len % ATT_TILE == 0
    n_p, n_s = x_prompt.shape[0], x_sample.shape[0]
    assert n_p + n_s <= SEQ_PAD
    n_chunks = seq_len // CHUNK

    w_in_bf = w_in[0].astype(BF16)
    tab = _bias_tables(rel_bias)
    ssm_wts = _ssm_weights(
        ssm_a_re[0], ssm_a_im[0], ssm_log_dt[0], ssm_b_re[0], ssm_b_im[0],
        ssm_c_re[0], ssm_c_im[0], ssm_d[0])
    row = lambda v, n: v.reshape(1, n)
    wts = (w_glu[0].astype(BF16), row(b_glu[0], SSM_WIDTH), row(g_att_out[0], ATT_WIDTH),
           row(g_ssm_out[0], SSM_WIDTH), w_out[0].astype(BF16), row(g_mlp[0], D_MODEL),
           w_mlp1[0].astype(BF16), w_mlp2[0].astype(BF16), row(g_ple[0], D_MODEL),
           w_ple_gate[0].astype(BF16), w_ple_proj[0].astype(BF16), row(g_final, D_MODEL))

    groups = []
    for x3, p4 in ((x_prompt, p_prompt), (x_sample, p_sample)):
        n = x3.shape[0]
        x = x3.reshape(n * seq_len, D_MODEL)
        qkv, ut = _inproj(x, g_mix[0], w_in_bf)
        att = _attention(qkv, tab, n, seq_len)
        groups.append((x, p4[0].reshape(n * seq_len, PLE_DIM), att, ut))
    yts = _ssm(groups[0][3], groups[1][3], *ssm_wts, n_chunks)
    outs = [_post(x, att, yt, p, wts).reshape(-1, seq_len, D_MODEL)
            for (x, p, att, _), yt in zip(groups, yts)]
    return outs[0], outs[1]
```

```python
import functools
import math

import jax
import jax.numpy as jnp
import numpy as np
from jax import lax
from jax.experimental import pallas as pl
from jax.experimental.pallas import tpu as pltpu

F32 = jnp.float32
BF16 = jnp.bfloat16

D_MODEL = 1024
ATT_HEADS = 8
HEAD_DIM = 64
ATT_WIDTH = ATT_HEADS * HEAD_DIM
SSM_WIDTH = D_MODEL - ATT_WIDTH
SSM_GROUP_CH = 16
SSM_GROUPS = SSM_WIDTH // SSM_GROUP_CH
SSM_STATE = 64
IN_COLS = 3 * ATT_WIDTH + SSM_WIDTH
D_FF = 4 * D_MODEL
PLE_DIM = 256
NUM_BUCKETS = 32
REL_MAX_DISTANCE = 1024
DIL_WINDOWS = (128, 512, 2048)
DIL_RATES = (1, 4, 16)
RMS_EPS = 1e-6
NEG_INF = -1e30
LOG2_E = math.log2(math.e)

LANES = 128
SUBLANES = 8
VMEM_LIMIT = 56 * 1024 * 1024

RADIUS = 64
Q_BLK = 128
K_BLK = Q_BLK + 2 * RADIUS
ATT_TILE = 2048
PHASES = 4
HEADS_PER_STEP = LANES // HEAD_DIM
ATT_GROUP = 16
N_VARIANTS = 3

CHUNK = 16
CHUNK_COLS = CHUNK * SSM_GROUP_CH
SEQ_PAD = SUBLANES

SUPER_TILE = CHUNK * LANES
IN_TILE = 512
POST_TILE = 512
FF_BLK = 1024


def _rms(x, g):
    return x * lax.rsqrt(jnp.mean(x * x, axis=-1, keepdims=True) + RMS_EPS) * g


def _sigmoid(x):
    return 1.0 / (1.0 + jnp.exp(-x))


def _gelu_tanh(x):
    c = math.sqrt(2.0 / math.pi)
    return 0.5 * x * (1.0 + jnp.tanh(c * (x + 0.044715 * (x * x * x))))


def _bdot(a, b):
    return jnp.dot(a.astype(BF16), b.astype(BF16), preferred_element_type=F32)


def _inproj_kernel(x_ref, g_ref, w_ref, qkv_ref, ut_ref, u_scr):
    k = pl.program_id(1)
    a = _rms(x_ref[...], g_ref[...])
    z = _bdot(a, w_ref[...])
    qkv_ref[...] = z[:, :3 * ATT_WIDTH]
    rows = pl.ds(pl.multiple_of(k * IN_TILE, IN_TILE), IN_TILE)
    for l in range(SSM_WIDTH // LANES):
        u_scr[l, rows, :] = z[:, 3 * ATT_WIDTH + l * LANES:3 * ATT_WIDTH + (l + 1) * LANES]

    @pl.when(k == SUPER_TILE // IN_TILE - 1)
    def _():
        for l in range(SSM_WIDTH // LANES):
            for j in range(CHUNK):
                ut_ref[j, l * LANES:(l + 1) * LANES, :] = u_scr.at[l][pl.ds(j, LANES, stride=CHUNK), :].T


def _inproj(x, g_mix, w_in_bf):
    t = x.shape[0]
    n_in = SUPER_TILE // IN_TILE
    return pl.pallas_call(
        _inproj_kernel,
        grid=(t // SUPER_TILE, n_in),
        in_specs=[
            pl.BlockSpec((IN_TILE, D_MODEL), lambda i, k: (i * n_in + k, 0)),
            pl.BlockSpec((1, D_MODEL), lambda i, k: (0, 0)),
            pl.BlockSpec((D_MODEL, IN_COLS), lambda i, k: (0, 0)),
        ],
        out_specs=[
            pl.BlockSpec((IN_TILE, 3 * ATT_WIDTH), lambda i, k: (i * n_in + k, 0)),
            pl.BlockSpec((CHUNK, SSM_WIDTH, LANES), lambda i, k: (0, 0, i)),
        ],
        out_shape=[
            jax.ShapeDtypeStruct((t, 3 * ATT_WIDTH), F32),
            jax.ShapeDtypeStruct((CHUNK, SSM_WIDTH, t // CHUNK), F32),
        ],
        scratch_shapes=[pltpu.VMEM((SSM_WIDTH // LANES, SUPER_TILE, LANES), F32)],
        compiler_params=pltpu.CompilerParams(
            dimension_semantics=("arbitrary", "arbitrary"), vmem_limit_bytes=VMEM_LIMIT),
        name="inproj",
    )(x, g_mix.reshape(1, D_MODEL), w_in_bf)


def _t5_bucket_np(rel):
    half = NUM_BUCKETS // 2
    n = -rel
    ret = np.where(n < 0, half, 0)
    n = np.abs(n)
    max_exact = half // 2
    nf = np.maximum(n, 1).astype(np.float64)
    large = max_exact + (np.log(nf / max_exact) / math.log(REL_MAX_DISTANCE / max_exact)
                         * (half - max_exact)).astype(np.int64)
    large = np.minimum(large, half - 1)
    return ret + np.where(n < max_exact, n, large)


def _bucket_tables():
    out = np.zeros((len(DIL_RATES), N_VARIANTS, Q_BLK, K_BLK), np.int32)
    for b, d in enumerate(DIL_RATES):
        qi = np.arange(Q_BLK)
        ci = np.arange(K_BLK)
        if d == 1:
            qi = (qi % (Q_BLK // PHASES)) * PHASES + qi // (Q_BLK // PHASES)
            ci = (ci % (K_BLK // PHASES)) * PHASES + ci // (K_BLK // PHASES)
        for v, shift in enumerate((0, -RADIUS, -2 * RADIUS)):
            off = ci[None, :] - qi[:, None] + shift
            bk = _t5_bucket_np(off * d)
            out[b, v] = np.where(np.abs(off) <= RADIUS, bk, NUM_BUCKETS)
    return out.reshape(len(DIL_RATES) * N_VARIANTS, Q_BLK, K_BLK)


def _bias_kernel(rel_ref, bk_ref, tab_ref):
    hp = pl.program_id(0)
    n_bv = bk_ref.shape[0]
    for bv in range(n_bv):
        bk = bk_ref[bv]
        for h2 in range(HEADS_PER_STEP):
            acc = jnp.full(bk.shape, NEG_INF, F32)
            for b in range(NUM_BUCKETS):
                acc = jnp.where(bk == b, rel_ref[b, hp * HEADS_PER_STEP + h2] * LOG2_E, acc)
            tab_ref[0, bv * HEADS_PER_STEP + h2] = acc


def _bias_tables(rel_bias):
    bk = jnp.asarray(_bucket_tables())
    n_bv = bk.shape[0]
    n_hp = ATT_HEADS // HEADS_PER_STEP
    return pl.pallas_call(
        _bias_kernel,
        grid=(n_hp,),
        in_specs=[
            pl.BlockSpec(memory_space=pltpu.SMEM),
            pl.BlockSpec((n_bv, Q_BLK, K_BLK), lambda h: (0, 0, 0)),
        ],
        out_specs=pl.BlockSpec((1, n_bv * HEADS_PER_STEP, Q_BLK, K_BLK), lambda h: (h, 0, 0, 0)),
        out_shape=jax.ShapeDtypeStruct((n_hp, n_bv * HEADS_PER_STEP, Q_BLK, K_BLK), F32),
        compiler_params=pltpu.CompilerParams(dimension_semantics=("arbitrary",)),
        name="bias_tables",
    )(rel_bias, bk)


def _att_kernel(q_ref, k_ref, v_ref, tab_ref, o_ref, k4_scr, va_scr, vb_scr, qa_scr, qb_scr,
                acc_scr, m_scr, den_scr, s_scr, p_scr, *, seq_len):
    t = pl.program_id(2)
    n4 = seq_len // PHASES
    t4 = ATT_TILE // PHASES
    n_sub = ATT_TILE // Q_BLK
    stage = 256

    def head0_mask(rows):
        return lax.broadcasted_iota(jnp.int32, (rows, LANES), 1) < HEAD_DIM

    head0 = head0_mask(Q_BLK)
    head0_s = head0_mask(stage)

    @pl.when(t == 0)
    def _():
        def body(c, carry):
            for r in range(PHASES):
                src = pl.ds(r + PHASES * stage * c, stage, stride=PHASES)
                dst = pl.ds(pl.multiple_of(stage * c, stage), stage)
                k4_scr[r, dst, :] = k_ref[src, :]
                v = v_ref[src, :]
                va_scr[r, dst, :] = jnp.where(head0_s, v, 1.0)
                vb_scr[r, dst, :] = jnp.where(head0_s, 1.0, v)
            return carry
        lax.fori_loop(0, n4 // stage, body, 0)

    for r in range(PHASES):
        for c in range(t4 // stage):
            q = q_ref[pl.ds(r + PHASES * stage * c, stage, stride=PHASES), :] * (LOG2_E * HEAD_DIM ** -0.5)
            qa_scr[r, c * stage:(c + 1) * stage, :] = jnp.where(head0_s, q, 0.0)
            qb_scr[r, c * stage:(c + 1) * stage, :] = jnp.where(head0_s, 0.0, q)

    def pieces(scr, start, n):
        return jnp.concatenate([scr[r, pl.ds(start, n), :] for r in range(PHASES)], axis=0)

    def sub_block(b, idx):
        d = DIL_RATES[b]
        n_m = seq_len // d
        if d == 1:
            m0 = t * ATT_TILE + Q_BLK * idx
        elif d == PHASES:
            r, blk = idx % PHASES, idx // PHASES
            m0 = t * t4 + Q_BLK * blk
        else:
            r, r16 = idx % PHASES, idx // PHASES
            m0 = t * (ATT_TILE // d)
        k_start = jnp.clip(m0 - RADIUS, 0, n_m - K_BLK)
        variant = jnp.where(m0 < RADIUS, 0, jnp.where(m0 > n_m - Q_BLK - RADIUS, 2, 1))
        if d == 1:
            q0 = (Q_BLK // PHASES) * idx
            k0 = pl.multiple_of(k_start // PHASES, SUBLANES)
            load_q = lambda scr: pieces(scr, q0, Q_BLK // PHASES)
            load_k = lambda scr: pieces(scr, k0, K_BLK // PHASES)

            def store(scr, val):
                n = Q_BLK // PHASES
                for rr in range(PHASES):
                    scr[b, rr, q0:q0 + n, :] = val[rr * n:(rr + 1) * n]
        elif d == PHASES:
            load_q = lambda scr: scr[r, blk * Q_BLK:(blk + 1) * Q_BLK, :]
            load_k = lambda scr: scr[r, pl.ds(pl.multiple_of(k_start, SUBLANES), K_BLK), :]

            def store(scr, val):
                scr[b, r, blk * Q_BLK:(blk + 1) * Q_BLK, :] = val
        else:
            e = d // PHASES
            load_q = lambda scr: scr.at[r][pl.ds(r16, Q_BLK, stride=e), :]
            load_k = lambda scr: scr.at[r][pl.ds(r16 + e * k_start, K_BLK, stride=e), :]

            def store(scr, val):
                scr.at[b, r][pl.ds(r16, Q_BLK, stride=e), :] = val
        return load_q, load_k, store, variant

    for b in range(len(DIL_RATES)):
        for g0 in range(0, n_sub, ATT_GROUP):
            stores, key_loaders = [], []
            for j in range(ATT_GROUP):
                load_q, load_k, store, variant = sub_block(b, g0 + j)
                stores.append(store)
                key_loaders.append(load_k)
                kb = load_k(k4_scr).astype(BF16)
                for h2, q_scr in enumerate((qa_scr, qb_scr)):
                    s = lax.dot_general(load_q(q_scr).astype(BF16), kb, (((1,), (1,)), ((), ())),
                                        preferred_element_type=F32)
                    s_scr[j * HEADS_PER_STEP + h2] = (
                        s + tab_ref[0, (b * N_VARIANTS + variant) * HEADS_PER_STEP + h2])
            for j in range(ATT_GROUP):
                ms = []
                for h2 in range(HEADS_PER_STEP):
                    s = s_scr[j * HEADS_PER_STEP + h2]
                    m = jnp.max(s, axis=-1, keepdims=True)
                    p_scr[j * HEADS_PER_STEP + h2] = jnp.exp2(s - m).astype(BF16)
                    ms.append(m)
                stores[j](m_scr, jnp.where(head0, ms[0], ms[1]))
            for j in range(ATT_GROUP):
                outs = [jnp.dot(p_scr[j * HEADS_PER_STEP + h2], key_loaders[j](vh_scr).astype(BF16),
                                preferred_element_type=F32)
                        for h2, vh_scr in enumerate((va_scr, vb_scr))]
                stores[j](acc_scr, jnp.where(head0, outs[0], outs[1]))
                stores[j](den_scr, jnp.where(head0, outs[1], outs[0]))

    for r in range(PHASES):
        m_all = m_scr[:, r]
        m = jnp.max(m_all, axis=0)
        num = jnp.zeros((t4, LANES), F32)
        den = jnp.zeros((t4, LANES), F32)
        for b in range(len(DIL_RATES)):
            w = jnp.exp2(m_all[b] - m)
            num = num + w * acc_scr[b, r]
            den = den + w * pltpu.roll(den_scr[b, r], HEAD_DIM, axis=1)
        o_ref[pl.ds(r, t4, stride=PHASES), :] = num / den


def _attention(qkv, tab, n_seq, seq_len):
    n_hp = ATT_HEADS // HEADS_PER_STEP
    n_t = seq_len // ATT_TILE
    n_br = len(DIL_RATES)
    return pl.pallas_call(
        functools.partial(_att_kernel, seq_len=seq_len),
        grid=(n_seq, n_hp, n_t),
        in_specs=[
            pl.BlockSpec((ATT_TILE, LANES), lambda s, h, t: (s * n_t + t, h)),
            pl.BlockSpec((seq_len, LANES), lambda s, h, t: (s, n_hp + h)),
            pl.BlockSpec((seq_len, LANES), lambda s, h, t: (s, 2 * n_hp + h)),
            pl.BlockSpec((1,) + tab.shape[1:], lambda s, h, t: (h, 0, 0, 0)),
        ],
        out_specs=pl.BlockSpec((ATT_TILE, LANES), lambda s, h, t: (s * n_t + t, h)),
        out_shape=jax.ShapeDtypeStruct((n_seq * seq_len, ATT_WIDTH), F32),
        scratch_shapes=[
            pltpu.VMEM((PHASES, seq_len // PHASES, LANES), F32),
            pltpu.VMEM((PHASES, seq_len // PHASES, LANES), F32),
            pltpu.VMEM((PHASES, seq_len // PHASES, LANES), F32),
            pltpu.VMEM((PHASES, ATT_TILE // PHASES, LANES), F32),
            pltpu.VMEM((PHASES, ATT_TILE // PHASES, LANES), F32),
            pltpu.VMEM((n_br, PHASES, ATT_TILE // PHASES, LANES), F32),
            pltpu.VMEM((n_br, PHASES, ATT_TILE // PHASES, LANES), F32),
            pltpu.VMEM((n_br, PHASES, ATT_TILE // PHASES, LANES), F32),
            pltpu.VMEM((ATT_GROUP * HEADS_PER_STEP, Q_BLK, K_BLK), F32),
            pltpu.VMEM((ATT_GROUP * HEADS_PER_STEP, Q_BLK, K_BLK), BF16),
        ],
        compiler_params=pltpu.CompilerParams(
            dimension_semantics=("arbitrary", "arbitrary", "arbitrary"),
            vmem_limit_bytes=VMEM_LIMIT),
        name="dilated_attention",
    )(qkv, qkv, qkv, tab)


def _ssm_weights(a_re, a_im, log_dt, b_re, b_im, c_re, c_im, d_skip):
    hi = lax.Precision.HIGHEST
    G, N, HC = SSM_GROUPS, SSM_STATE, SSM_GROUP_CH
    dt = jnp.exp(log_dt)[..., None]
    mag = jnp.exp(a_re * dt)
    ab_re = mag * jnp.cos(a_im * dt)
    ab_im = mag * jnp.sin(a_im * dt)
    inv = 1.0 / (a_re * a_re + a_im * a_im)
    f_re = ((ab_re - 1.0) * a_re + ab_im * a_im) * inv
    f_im = (ab_im * a_re - (ab_re - 1.0) * a_im) * inv
    bb_re = f_re[..., None] * b_re - f_im[..., None] * b_im
    bb_im = f_re[..., None] * b_im + f_im[..., None] * b_re
    tau = jnp.arange(CHUNK + 1, dtype=F32)
    mag_t = jnp.exp((a_re * dt)[..., None] * tau)
    pr = mag_t * jnp.cos((a_im * dt)[..., None] * tau)
    pi = mag_t * jnp.sin((a_im * dt)[..., None] * tau)

    def a_pow_b(direction, rev):
        order = slice(CHUNK - 1, None, -1) if rev else slice(0, CHUNK)
        pr_c = jnp.repeat(pr[direction][..., order], HC, axis=-1)
        pi_c = jnp.repeat(pi[direction][..., order], HC, axis=-1)
        br = jnp.tile(bb_re[direction], (1, 1, CHUNK))
        bi = jnp.tile(bb_im[direction], (1, 1, CHUNK))
        return pr_c * br - pi_c * bi, pr_c * bi + pi_c * br

    def lag_table(direction, ab):
        return (jnp.einsum('gcn,gnx->gcx', c_re[direction], ab[0], precision=hi)
                - jnp.einsum('gcn,gnx->gcx', c_im[direction], ab[1], precision=hi))

    ab_f = a_pow_b(0, True)
    ab_b = a_pow_b(1, False)
    lag = jnp.stack([lag_table(0, ab_f), lag_table(1, ab_b)], axis=1)
    w_state = jnp.concatenate(
        [ab_f[0], ab_f[1], ab_f[1], ab_f[0], ab_b[0], ab_b[1], ab_b[1], ab_b[0]], axis=1).astype(BF16)
    pr = jnp.moveaxis(pr, -1, 0)
    pi = jnp.moveaxis(pi, -1, 0)

    def state_out(direction, pr_sel, pi_sel):
        cr = c_re[direction][:, None, :, :]
        ci = c_im[direction][:, None, :, :]
        pr_s = pr_sel.transpose(1, 0, 2)[:, :, None, :]
        pi_s = pi_sel.transpose(1, 0, 2)[:, :, None, :]
        from_re = (cr * pr_s - ci * pi_s).reshape(G, CHUNK_COLS, N)
        from_im = (-cr * pi_s - ci * pr_s).reshape(G, CHUNK_COLS, N)
        return jnp.concatenate([from_re, from_im], axis=2)

    c_pow = jnp.concatenate(
        [state_out(0, pr[1:CHUNK + 1, 0], pi[1:CHUNK + 1, 0]),
         state_out(1, pr[CHUNK:0:-1, 1], pi[CHUNK:0:-1, 1])], axis=2).astype(BF16)

    def packed(direction):
        ar, ai = pr[CHUNK, direction], pi[CHUNK, direction]
        return [jnp.concatenate([ar, ar], -1), jnp.concatenate([-ai, ai], -1),
                jnp.concatenate([ai, -ai], -1)]
    a_chunk = jnp.stack(packed(0) + packed(1), axis=1)
    d_col = jnp.tile(d_skip[:, None, :], (1, CHUNK, 1)).reshape(G, CHUNK_COLS, 1)
    return lag, w_state, c_pow, a_chunk, d_col


def _ssm_kernel(utp_ref, uts_ref, lag_ref, ws_ref, cp_ref, a_ref, d_ref, ytp_ref, yts_ref, g_scr, h_scr,
                *, n_chunks, n_p, n_s):
    seqs = [(utp_ref, ytp_ref, s) for s in range(n_p)] + [(uts_ref, yts_ref, s) for s in range(n_s)]
    n_state = 2 * SSM_STATE

    @pl.when(pl.program_id(0) == 0)
    def _():
        g_scr[...] = jnp.zeros(g_scr.shape, F32)

    lag_f, lag_b = lag_ref[0, 0], lag_ref[0, 1]
    col_j = lax.broadcasted_iota(jnp.int32, (SSM_GROUP_CH, CHUNK_COLS), 1) // SSM_GROUP_CH
    blocks = []
    for i in range(CHUNK):
        fwd = pltpu.roll(lag_f, (i + 1) * SSM_GROUP_CH % CHUNK_COLS, axis=1)
        bwd = pltpu.roll(lag_b, i * SSM_GROUP_CH, axis=1)
        blocks.append(jnp.where(col_j <= i, fwd, 0.0) + jnp.where(col_j >= i, bwd, 0.0))
    w_intra = jnp.concatenate(blocks, axis=0).astype(BF16)
    w_state = ws_ref[0]

    for slot, (u_ref, y_ref, s) in enumerate(seqs):
        lanes = slice(s * n_chunks, (s + 1) * n_chunks)
        x_t = u_ref[:, :, lanes].reshape(CHUNK_COLS, n_chunks).astype(BF16)
        r = jnp.dot(w_intra, x_t, preferred_element_type=F32)
        y_ref[:, :, lanes] = r.reshape(CHUNK, SSM_GROUP_CH, n_chunks)
        r = jnp.dot(w_state, x_t, preferred_element_type=F32)
        for k in range(4):
            g_scr.at[k][pl.ds(slot, n_chunks, stride=SEQ_PAD), :] = r[k * n_state:(k + 1) * n_state].T

    a = a_ref[0]
    shape = (SEQ_PAD, LANES)
    pf, qf, q2f, pb, qb, q2b = [jnp.broadcast_to(a[k:k + 1], shape) for k in range(6)]

    def step(c, carry):
        hf1, hf2, hb1, hb2 = carry
        rf = pl.ds(pl.multiple_of(c * SEQ_PAD, SEQ_PAD), SEQ_PAD)
        rb = pl.ds(pl.multiple_of((n_chunks - 1 - c) * SEQ_PAD, SEQ_PAD), SEQ_PAD)
        h_scr[0, rf, :] = hf1
        h_scr[1, rb, :] = hb1
        return (pf * hf1 + qf * hf2 + g_scr[0, rf, :], pf * hf2 + q2f * hf1 + g_scr[1, rf, :],
                pb * hb1 + qb * hb2 + g_scr[2, rb, :], pb * hb2 + q2b * hb1 + g_scr[3, rb, :])

    zero = jnp.zeros(shape, F32)
    lax.fori_loop(0, n_chunks, step, (zero, zero, zero, zero))

    cp = cp_ref[0]
    d_col = d_ref[0]
    for slot, (u_ref, y_ref, s) in enumerate(seqs):
        lanes = slice(s * n_chunks, (s + 1) * n_chunks)
        h_t = jnp.concatenate(
            [h_scr.at[k][pl.ds(slot, n_chunks, stride=SEQ_PAD), :].T for k in range(2)], axis=0)
        x_t = u_ref[:, :, lanes].reshape(CHUNK_COLS, n_chunks)
        y = (y_ref[:, :, lanes].reshape(CHUNK_COLS, n_chunks)
             + jnp.dot(cp, h_t.astype(BF16), preferred_element_type=F32) + d_col * x_t)
        y_ref[:, :, lanes] = y.reshape(CHUNK, SSM_GROUP_CH, n_chunks)


def _ssm(ut_p, ut_s, lag, w_state, c_pow, a_chunk, d_col, n_chunks):
    G = SSM_GROUPS
    n_p = ut_p.shape[2] // n_chunks
    n_s = ut_s.shape[2] // n_chunks
    n_rows = n_chunks * SEQ_PAD
    act = lambda arr: pl.BlockSpec((CHUNK, SSM_GROUP_CH, arr.shape[2]), lambda g: (0, g, 0))
    per_group = lambda arr: pl.BlockSpec((1,) + arr.shape[1:], lambda g: (g,) + (0,) * (arr.ndim - 1))
    return pl.pallas_call(
        functools.partial(_ssm_kernel, n_chunks=n_chunks, n_p=n_p, n_s=n_s),
        grid=(G,),
        in_specs=[act(ut_p), act(ut_s), per_group(lag), per_group(w_state), per_group(c_pow),
                  per_group(a_chunk), per_group(d_col)],
        out_specs=[act(ut_p), act(ut_s)],
        out_shape=[jax.ShapeDtypeStruct(ut_p.shape, F32), jax.ShapeDtypeStruct(ut_s.shape, F32)],
        scratch_shapes=[
            pltpu.VMEM((4, n_rows, LANES), F32),
            pltpu.VMEM((2, n_rows, LANES), F32),
        ],
        compiler_params=pltpu.CompilerParams(
            dimension_semantics=("arbitrary",), vmem_limit_bytes=VMEM_LIMIT),
        name="ssm_chunked",
    )(ut_p, ut_s, lag, w_state, c_pow, a_chunk, d_col)


def _post_kernel(x_ref, att_ref, yt_ref, p_ref, wglu_ref, bglu_ref, gatt_ref, gssm_ref, wout_ref,
                 gmlp_ref, w1_ref, w2_ref, gple_ref, wgate_ref, wproj_ref, gfin_ref, o_ref, ys_scr):
    k = pl.program_id(1)

    @pl.when(k == 0)
    def _():
        for l in range(SSM_WIDTH // LANES):
            for i in range(CHUNK):
                ys_scr.at[l][pl.ds(i, LANES, stride=CHUNK), :] = yt_ref[i, l * LANES:(l + 1) * LANES, :].T

    rows = pl.ds(pl.multiple_of(k * POST_TILE, POST_TILE), POST_TILE)
    ys = jnp.concatenate([ys_scr[l, rows, :] for l in range(SSM_WIDTH // LANES)], axis=1)
    g = _gelu_tanh(ys)
    ssm = g * _sigmoid(_bdot(g, wglu_ref[...]) + bglu_ref[...])
    att_n = _rms(att_ref[...], gatt_ref[...])
    ssm_n = _rms(ssm, gssm_ref[...])
    h = x_ref[...] + (_bdot(att_n, wout_ref[:ATT_WIDTH, :]) + _bdot(ssm_n, wout_ref[ATT_WIDTH:, :]))
    f = _rms(h, gmlp_ref[...]).astype(BF16)
    acc = jnp.zeros_like(h)
    for kb in range(D_FF // FF_BLK):
        cols = slice(kb * FF_BLK, (kb + 1) * FF_BLK)
        t = jnp.dot(f, w1_ref[:, cols], preferred_element_type=F32)
        t = jnp.square(jnp.maximum(t, 0.0))
        acc = acc + _bdot(t, w2_ref[cols, :])
    h = h + acc
    e = _rms(h, gple_ref[...])
    h = h + _sigmoid(_bdot(e, wgate_ref[...])) * _bdot(p_ref[...], wproj_ref[...])
    o_ref[...] = _rms(h, gfin_ref[...])


def _post(x, att, yt, p, wts):
    t = x.shape[0]
    n_in = SUPER_TILE // POST_TILE
    tile = lambda width: pl.BlockSpec((POST_TILE, width), lambda i, k: (i * n_in + k, 0))

    def resident(arr):
        return pl.BlockSpec(arr.shape, lambda i, k: (0,) * arr.ndim, pipeline_mode=pl.Buffered(1))

    return pl.pallas_call(
        _post_kernel,
        grid=(t // SUPER_TILE, n_in),
        in_specs=[tile(D_MODEL), tile(ATT_WIDTH),
                  pl.BlockSpec((CHUNK, SSM_WIDTH, LANES), lambda i, k: (0, 0, i)), tile(PLE_DIM)]
                 + [resident(w) for w in wts],
        out_specs=tile(D_MODEL),
        out_shape=jax.ShapeDtypeStruct((t, D_MODEL), F32),
        scratch_shapes=[pltpu.VMEM((SSM_WIDTH // LANES, SUPER_TILE, LANES), F32)],
        compiler_params=pltpu.CompilerParams(
            dimension_semantics=("arbitrary", "arbitrary"), vmem_limit_bytes=VMEM_LIMIT),
        name="post_mixers",
    )(x, att, yt, p, *wts)


def kernel(x_prompt, x_sample, p_prompt, p_sample, rel_bias, g_mix, w_in, ssm_a_re, ssm_a_im, ssm_log_dt, ssm_b_re, ssm_b_im, ssm_c_re, ssm_c_im, ssm_d, w_glu, b_glu, g_att_out, g_ssm_out, w_out, g_mlp, w_mlp1, w_mlp2, g_ple, w_ple_gate, w_ple_proj, g_final):
    assert g_mix.shape[0] == 1, "single-layer trunk"
    seq_len = x_prompt.shape[1]
    assert x_sample.shape[1] == seq_len and seq_len % ATT_TILE == 0
    n_p, n_s = x_prompt.shape[0], x_sample.shape[0]
    assert n_p + n_s <= SEQ_PAD
    n_chunks = seq_len // CHUNK

    w_in_bf = w_in[0].astype(BF16)
    tab = _bias_tables(rel_bias)
    ssm_wts = _ssm_weights(
        ssm_a_re[0], ssm_a_im[0], ssm_log_dt[0], ssm_b_re[0], ssm_b_im[0],
        ssm_c_re[0], ssm_c_im[0], ssm_d[0])
    row = lambda v, n: v.reshape(1, n)
    wts = (w_glu[0].astype(BF16), row(b_glu[0], SSM_WIDTH), row(g_att_out[0], ATT_WIDTH),
           row(g_ssm_out[0], SSM_WIDTH), w_out[0].astype(BF16), row(g_mlp[0], D_MODEL),
           w_mlp1[0].astype(BF16), w_mlp2[0].astype(BF16), row(g_ple[0], D_MODEL),
           w_ple_gate[0].astype(BF16), w_ple_proj[0].astype(BF16), row(g_final, D_MODEL))

    groups = []
    for x3, p4 in ((x_prompt, p_prompt), (x_sample, p_sample)):
        n = x3.shape[0]
        x = x3.reshape(n * seq_len, D_MODEL)
        qkv, ut = _inproj(x, g_mix[0], w_in_bf)
        att = _attention(qkv, tab, n, seq_len)
        groups.append((x, p4[0].reshape(n * seq_len, PLE_DIM), att, ut))
    yts = _ssm(groups[0][3], groups[1][3], *ssm_wts, n_chunks)
    outs = [_post(x, att, yt, p, wts).reshape(-1, seq_len, D_MODEL)
            for (x, p, att, _), yt in zip(groups, yts)]
    return outs[0], outs[1]
```

```python
import functools
import math

import jax
import jax.numpy as jnp
import numpy as np
from jax import lax
from jax.experimental import pallas as pl
from jax.experimental.pallas import tpu as pltpu

F32 = jnp.float32
BF16 = jnp.bfloat16

D_MODEL = 1024
ATT_HEADS = 8
HEAD_DIM = 64
ATT_WIDTH = ATT_HEADS * HEAD_DIM
SSM_WIDTH = D_MODEL - ATT_WIDTH
SSM_GROUP_CH = 16
SSM_GROUPS = SSM_WIDTH // SSM_GROUP_CH
SSM_STATE = 64
IN_COLS = 3 * ATT_WIDTH + SSM_WIDTH
D_FF = 4 * D_MODEL
PLE_DIM = 256
NUM_BUCKETS = 32
REL_MAX_DISTANCE = 1024
DIL_WINDOWS = (128, 512, 2048)
DIL_RATES = (1, 4, 16)
RMS_EPS = 1e-6
NEG_INF = -1e30
LOG2_E = math.log2(math.e)

LANES = 128
SUBLANES = 8
VMEM_LIMIT = 56 * 1024 * 1024

RADIUS = 64
Q_BLK = 128
K_BLK = Q_BLK + 2 * RADIUS
ATT_TILE = 2048
PHASES = 4
HEADS_PER_STEP = LANES // HEAD_DIM
ATT_GROUP = 16
N_VARIANTS = 3

CHUNK = 16
CHUNK_COLS = CHUNK * SSM_GROUP_CH
SEQ_PAD = SUBLANES
SSM_PAIR = LANES // SSM_STATE

SUPER_TILE = CHUNK * LANES
IN_TILE = 1024
POST_TILE = 512
FF_BLK = 1024


def _rms(x, g):
    return x * lax.rsqrt(jnp.mean(x * x, axis=-1, keepdims=True) + RMS_EPS) * g


def _sigmoid(x):
    return 1.0 / (1.0 + jnp.exp(-x))


def _gelu_tanh(x):
    c = math.sqrt(2.0 / math.pi)
    return 0.5 * x * (1.0 + jnp.tanh(c * (x + 0.044715 * (x * x * x))))


def _bdot(a, b):
    return jnp.dot(a.astype(BF16), b.astype(BF16), preferred_element_type=F32)


def _inproj_kernel(x_ref, g_ref, w_ref, qkv_ref, ut_ref, u_scr):
    k = pl.program_id(1)
    a = _rms(x_ref[...], g_ref[...])
    z = _bdot(a, w_ref[...])
    qkv_ref[...] = z[:, :3 * ATT_WIDTH]
    rows = pl.ds(pl.multiple_of(k * IN_TILE, IN_TILE), IN_TILE)
    for l in range(SSM_WIDTH // LANES):
        u_scr[l, rows, :] = z[:, 3 * ATT_WIDTH + l * LANES:3 * ATT_WIDTH + (l + 1) * LANES]

    @pl.when(k == SUPER_TILE // IN_TILE - 1)
    def _():
        for l in range(SSM_WIDTH // LANES):
            for j in range(CHUNK):
                ut_ref[j, l * LANES:(l + 1) * LANES, :] = u_scr.at[l][pl.ds(j, LANES, stride=CHUNK), :].T


def _inproj(x, g_mix, w_in_bf):
    t = x.shape[0]
    n_in = SUPER_TILE // IN_TILE
    return pl.pallas_call(
        _inproj_kernel,
        grid=(t // SUPER_TILE, n_in),
        in_specs=[
            pl.BlockSpec((IN_TILE, D_MODEL), lambda i, k: (i * n_in + k, 0)),
            pl.BlockSpec((1, D_MODEL), lambda i, k: (0, 0)),
            pl.BlockSpec((D_MODEL, IN_COLS), lambda i, k: (0, 0)),
        ],
        out_specs=[
            pl.BlockSpec((IN_TILE, 3 * ATT_WIDTH), lambda i, k: (i * n_in + k, 0)),
            pl.BlockSpec((CHUNK, SSM_WIDTH, LANES), lambda i, k: (0, 0, i)),
        ],
        out_shape=[
            jax.ShapeDtypeStruct((t, 3 * ATT_WIDTH), F32),
            jax.ShapeDtypeStruct((CHUNK, SSM_WIDTH, t // CHUNK), F32),
        ],
        scratch_shapes=[pltpu.VMEM((SSM_WIDTH // LANES, SUPER_TILE, LANES), F32)],
        compiler_params=pltpu.CompilerParams(
            dimension_semantics=("arbitrary", "arbitrary"), vmem_limit_bytes=VMEM_LIMIT),
        name="inproj",
    )(x, g_mix.reshape(1, D_MODEL), w_in_bf)


def _t5_bucket_np(rel):
    half = NUM_BUCKETS // 2
    n = -rel
    ret = np.where(n < 0, half, 0)
    n = np.abs(n)
    max_exact = half // 2
    nf = np.maximum(n, 1).astype(np.float64)
    large = max_exact + (np.log(nf / max_exact) / math.log(REL_MAX_DISTANCE / max_exact)
                         * (half - max_exact)).astype(np.int64)
    large = np.minimum(large, half - 1)
    return ret + np.where(n < max_exact, n, large)


def _bucket_tables():
    out = np.zeros((len(DIL_RATES), N_VARIANTS, Q_BLK, K_BLK), np.int32)
    for b, d in enumerate(DIL_RATES):
        qi = np.arange(Q_BLK)
        ci = np.arange(K_BLK)
        if d == 1:
            qi = (qi % (Q_BLK // PHASES)) * PHASES + qi // (Q_BLK // PHASES)
            ci = (ci % (K_BLK // PHASES)) * PHASES + ci // (K_BLK // PHASES)
        for v, shift in enumerate((0, -RADIUS, -2 * RADIUS)):
            off = ci[None, :] - qi[:, None] + shift
            bk = _t5_bucket_np(off * d)
            out[b, v] = np.where(np.abs(off) <= RADIUS, bk, NUM_BUCKETS)
    return out.reshape(len(DIL_RATES) * N_VARIANTS, Q_BLK, K_BLK)


def _bias_kernel(rel_ref, bk_ref, tab_ref):
    hp = pl.program_id(0)
    n_bv = bk_ref.shape[0]
    for bv in range(n_bv):
        bk = bk_ref[bv]
        for h2 in range(HEADS_PER_STEP):
            acc = jnp.full(bk.shape, NEG_INF, F32)
            for b in range(NUM_BUCKETS):
                acc = jnp.where(bk == b, rel_ref[b, hp * HEADS_PER_STEP + h2] * LOG2_E, acc)
            tab_ref[0, bv * HEADS_PER_STEP + h2] = acc


def _bias_tables(rel_bias):
    bk = jnp.asarray(_bucket_tables())
    n_bv = bk.shape[0]
    n_hp = ATT_HEADS // HEADS_PER_STEP
    return pl.pallas_call(
        _bias_kernel,
        grid=(n_hp,),
        in_specs=[
            pl.BlockSpec(memory_space=pltpu.SMEM),
            pl.BlockSpec((n_bv, Q_BLK, K_BLK), lambda h: (0, 0, 0)),
        ],
        out_specs=pl.BlockSpec((1, n_bv * HEADS_PER_STEP, Q_BLK, K_BLK), lambda h: (h, 0, 0, 0)),
        out_shape=jax.ShapeDtypeStruct((n_hp, n_bv * HEADS_PER_STEP, Q_BLK, K_BLK), F32),
        compiler_params=pltpu.CompilerParams(dimension_semantics=("arbitrary",)),
        name="bias_tables",
    )(rel_bias, bk)


def _att_kernel(q_ref, k_ref, v_ref, tab_ref, o_ref, k4_scr, va_scr, vb_scr, qa_scr, qb_scr,
                acc_scr, m_scr, den_scr, s_scr, p_scr, *, seq_len):
    t = pl.program_id(2)
    n4 = seq_len // PHASES
    t4 = ATT_TILE // PHASES
    n_sub = ATT_TILE // Q_BLK
    stage = 256

    def head0_mask(rows):
        return lax.broadcasted_iota(jnp.int32, (rows, LANES), 1) < HEAD_DIM

    head0 = head0_mask(Q_BLK)
    head0_s = head0_mask(stage)

    @pl.when(t == 0)
    def _():
        def body(c, carry):
            for r in range(PHASES):
                src = pl.ds(r + PHASES * stage * c, stage, stride=PHASES)
                dst = pl.ds(pl.multiple_of(stage * c, stage), stage)
                k4_scr[r, dst, :] = k_ref[src, :]
                v = v_ref[src, :]
                va_scr[r, dst, :] = jnp.where(head0_s, v, 1.0)
                vb_scr[r, dst, :] = jnp.where(head0_s, 1.0, v)
            return carry
        lax.fori_loop(0, n4 // stage, body, 0)

    for r in range(PHASES):
        for c in range(t4 // stage):
            q = q_ref[pl.ds(r + PHASES * stage * c, stage, stride=PHASES), :] * (LOG2_E * HEAD_DIM ** -0.5)
            qa_scr[r, c * stage:(c + 1) * stage, :] = jnp.where(head0_s, q, 0.0)
            qb_scr[r, c * stage:(c + 1) * stage, :] = jnp.where(head0_s, 0.0, q)

    def pieces(scr, start, n):
        return jnp.concatenate([scr[r, pl.ds(start, n), :] for r in range(PHASES)], axis=0)

    def sub_block(b, idx):
        d = DIL_RATES[b]
        n_m = seq_len // d
        if d == 1:
            m0 = t * ATT_TILE + Q_BLK * idx
        elif d == PHASES:
            r, blk = idx % PHASES, idx // PHASES
            m0 = t * t4 + Q_BLK * blk
        else:
            r, r16 = idx % PHASES, idx // PHASES
            m0 = t * (ATT_TILE // d)
        k_start = jnp.clip(m0 - RADIUS, 0, n_m - K_BLK)
        variant = jnp.where(m0 < RADIUS, 0, jnp.where(m0 > n_m - Q_BLK - RADIUS, 2, 1))
        if d == 1:
            q0 = (Q_BLK // PHASES) * idx
            k0 = pl.multiple_of(k_start // PHASES, SUBLANES)
            load_q = lambda scr: pieces(scr, q0, Q_BLK // PHASES)
            load_k = lambda scr: pieces(scr, k0, K_BLK // PHASES)

            def store(scr, val):
                n = Q_BLK // PHASES
                for rr in range(PHASES):
                    scr[b, rr, q0:q0 + n, :] = val[rr * n:(rr + 1) * n]
        elif d == PHASES:
            load_q = lambda scr: scr[r, blk * Q_BLK:(blk + 1) * Q_BLK, :]
            load_k = lambda scr: scr[r, pl.ds(pl.multiple_of(k_start, SUBLANES), K_BLK), :]

            def store(scr, val):
                scr[b, r, blk * Q_BLK:(blk + 1) * Q_BLK, :] = val
        else:
            e = d // PHASES
            load_q = lambda scr: scr.at[r][pl.ds(r16, Q_BLK, stride=e), :]
            load_k = lambda scr: scr.at[r][pl.ds(r16 + e * k_start, K_BLK, stride=e), :]

            def store(scr, val):
                scr.at[b, r][pl.ds(r16, Q_BLK, stride=e), :] = val
        return load_q, load_k, store, variant

    for b in range(len(DIL_RATES)):
        for g0 in range(0, n_sub, ATT_GROUP):
            stores, key_loaders = [], []
            for j in range(ATT_GROUP):
                load_q, load_k, store, variant = sub_block(b, g0 + j)
                stores.append(store)
                key_loaders.append(load_k)
                kb = load_k(k4_scr).astype(BF16)
                for h2, q_scr in enumerate((qa_scr, qb_scr)):
                    s = lax.dot_general(load_q(q_scr).astype(BF16), kb, (((1,), (1,)), ((), ())),
                                        preferred_element_type=F32)
                    s_scr[j * HEADS_PER_STEP + h2] = (
                        s + tab_ref[0, (b * N_VARIANTS + variant) * HEADS_PER_STEP + h2])
            for j in range(ATT_GROUP):
                ms = []
                for h2 in range(HEADS_PER_STEP):
                    s = s_scr[j * HEADS_PER_STEP + h2]
                    m = jnp.max(s, axis=-1, keepdims=True)
                    p_scr[j * HEADS_PER_STEP + h2] = jnp.exp2(s - m).astype(BF16)
                    ms.append(m)
                stores[j](m_scr, jnp.where(head0, ms[0], ms[1]))
            for j in range(ATT_GROUP):
                outs = [jnp.dot(p_scr[j * HEADS_PER_STEP + h2], key_loaders[j](vh_scr).astype(BF16),
                                preferred_element_type=F32)
                        for h2, vh_scr in enumerate((va_scr, vb_scr))]
                stores[j](acc_scr, jnp.where(head0, outs[0], outs[1]))
                stores[j](den_scr, jnp.where(head0, outs[1], outs[0]))

    for r in range(PHASES):
        m_all = m_scr[:, r]
        m = jnp.max(m_all, axis=0)
        num = jnp.zeros((t4, LANES), F32)
        den = jnp.zeros((t4, LANES), F32)
        for b in range(len(DIL_RATES)):
            w = jnp.exp2(m_all[b] - m)
            num = num + w * acc_scr[b, r]
            den = den + w * pltpu.roll(den_scr[b, r], HEAD_DIM, axis=1)
        o_ref[pl.ds(r, t4, stride=PHASES), :] = num / den


def _attention(qkv, tab, n_seq, seq_len):
    n_hp = ATT_HEADS // HEADS_PER_STEP
    n_t = seq_len // ATT_TILE
    n_br = len(DIL_RATES)
    return pl.pallas_call(
        functools.partial(_att_kernel, seq_len=seq_len),
        grid=(n_seq, n_hp, n_t),
        in_specs=[
            pl.BlockSpec((ATT_TILE, LANES), lambda s, h, t: (s * n_t + t, h)),
            pl.BlockSpec((seq_len, LANES), lambda s, h, t: (s, n_hp + h)),
            pl.BlockSpec((seq_len, LANES), lambda s, h, t: (s, 2 * n_hp + h)),
            pl.BlockSpec((1,) + tab.shape[1:], lambda s, h, t: (h, 0, 0, 0)),
        ],
        out_specs=pl.BlockSpec((ATT_TILE, LANES), lambda s, h, t: (s * n_t + t, h)),
        out_shape=jax.ShapeDtypeStruct((n_seq * seq_len, ATT_WIDTH), F32),
        scratch_shapes=[
            pltpu.VMEM((PHASES, seq_len // PHASES, LANES), F32),
            pltpu.VMEM((PHASES, seq_len // PHASES, LANES), F32),
            pltpu.VMEM((PHASES, seq_len // PHASES, LANES), F32),
            pltpu.VMEM((PHASES, ATT_TILE // PHASES, LANES), F32),
            pltpu.VMEM((PHASES, ATT_TILE // PHASES, LANES), F32),
            pltpu.VMEM((n_br, PHASES, ATT_TILE // PHASES, LANES), F32),
            pltpu.VMEM((n_br, PHASES, ATT_TILE // PHASES, LANES), F32),
            pltpu.VMEM((n_br, PHASES, ATT_TILE // PHASES, LANES), F32),
            pltpu.VMEM((ATT_GROUP * HEADS_PER_STEP, Q_BLK, K_BLK), F32),
            pltpu.VMEM((ATT_GROUP * HEADS_PER_STEP, Q_BLK, K_BLK), BF16),
        ],
        compiler_params=pltpu.CompilerParams(
            dimension_semantics=("arbitrary", "arbitrary", "arbitrary"),
            vmem_limit_bytes=VMEM_LIMIT),
        name="dilated_attention",
    )(qkv, qkv, qkv, tab)


def _ssm_weights(a_re, a_im, log_dt, b_re, b_im, c_re, c_im, d_skip):
    hi = lax.Precision.HIGHEST
    G, N, HC = SSM_GROUPS, SSM_STATE, SSM_GROUP_CH
    dt = jnp.exp(log_dt)[..., None]
    mag = jnp.exp(a_re * dt)
    ab_re = mag * jnp.cos(a_im * dt)
    ab_im = mag * jnp.sin(a_im * dt)
    inv = 1.0 / (a_re * a_re + a_im * a_im)
    f_re = ((ab_re - 1.0) * a_re + ab_im * a_im) * inv
    f_im = (ab_im * a_re - (ab_re - 1.0) * a_im) * inv
    bb_re = f_re[..., None] * b_re - f_im[..., None] * b_im
    bb_im = f_re[..., None] * b_im + f_im[..., None] * b_re
    tau = jnp.arange(CHUNK + 1, dtype=F32)
    mag_t = jnp.exp((a_re * dt)[..., None] * tau)
    pr = mag_t * jnp.cos((a_im * dt)[..., None] * tau)
    pi = mag_t * jnp.sin((a_im * dt)[..., None] * tau)

    def a_pow_b(direction, rev):
        order = slice(CHUNK - 1, None, -1) if rev else slice(0, CHUNK)
        pr_c = jnp.repeat(pr[direction][..., order], HC, axis=-1)
        pi_c = jnp.repeat(pi[direction][..., order], HC, axis=-1)
        br = jnp.tile(bb_re[direction], (1, 1, CHUNK))
        bi = jnp.tile(bb_im[direction], (1, 1, CHUNK))
        return pr_c * br - pi_c * bi, pr_c * bi + pi_c * br

    def lag_table(direction, ab):
        return (jnp.einsum('gcn,gnx->gcx', c_re[direction], ab[0], precision=hi)
                - jnp.einsum('gcn,gnx->gcx', c_im[direction], ab[1], precision=hi))

    ab_f = a_pow_b(0, True)
    ab_b = a_pow_b(1, False)
    lag = jnp.stack([lag_table(0, ab_f), lag_table(1, ab_b)], axis=1)
    w_state = jnp.concatenate([ab_f[0], ab_f[1], ab_b[0], ab_b[1]], axis=1).astype(BF16)
    pr = jnp.moveaxis(pr, -1, 0)
    pi = jnp.moveaxis(pi, -1, 0)

    def state_out(direction, pr_sel, pi_sel):
        cr = c_re[direction][:, None, :, :]
        ci = c_im[direction][:, None, :, :]
        pr_s = pr_sel.transpose(1, 0, 2)[:, :, None, :]
        pi_s = pi_sel.transpose(1, 0, 2)[:, :, None, :]
        from_re = (cr * pr_s - ci * pi_s).reshape(G, CHUNK_COLS, N)
        from_im = (-cr * pi_s - ci * pr_s).reshape(G, CHUNK_COLS, N)
        return jnp.concatenate([from_re, from_im], axis=2)

    c_pow = jnp.concatenate(
        [state_out(0, pr[1:CHUNK + 1, 0], pi[1:CHUNK + 1, 0]),
         state_out(1, pr[CHUNK:0:-1, 1], pi[CHUNK:0:-1, 1])], axis=2).astype(BF16)

    a_chunk = jnp.stack([pr[CHUNK, 0], pi[CHUNK, 0], pr[CHUNK, 1], pi[CHUNK, 1]], axis=1)
    a_chunk = a_chunk.reshape(G // SSM_PAIR, SSM_PAIR, 4, N).transpose(0, 2, 1, 3).reshape(
        G // SSM_PAIR, 4, SSM_PAIR * N)
    d_col = jnp.tile(d_skip[:, None, :], (1, CHUNK, 1)).reshape(G, CHUNK_COLS, 1)
    return lag, w_state, c_pow, a_chunk, d_col


def _ssm_kernel(utp_ref, uts_ref, lag_ref, ws_ref, cp_ref, a_ref, d_ref, ytp_ref, yts_ref, g_scr, h_scr,
                *, n_chunks, n_p, n_s):
    seqs = [(utp_ref, ytp_ref, s) for s in range(n_p)] + [(uts_ref, yts_ref, s) for s in range(n_s)]
    n_st = SSM_STATE
    pair = range(SSM_PAIR)

    def group_rows(q):
        return slice(q * SSM_GROUP_CH, (q + 1) * SSM_GROUP_CH)

    @pl.when(pl.program_id(0) == 0)
    def _():
        g_scr[...] = jnp.zeros(g_scr.shape, F32)

    col_j = lax.broadcasted_iota(jnp.int32, (SSM_GROUP_CH, CHUNK_COLS), 1) // SSM_GROUP_CH
    w_intra = []
    for q in pair:
        lag_f, lag_b = lag_ref[q, 0], lag_ref[q, 1]
        blocks = []
        for i in range(CHUNK):
            fwd = pltpu.roll(lag_f, (i + 1) * SSM_GROUP_CH % CHUNK_COLS, axis=1)
            bwd = pltpu.roll(lag_b, i * SSM_GROUP_CH, axis=1)
            blocks.append(jnp.where(col_j <= i, fwd, 0.0) + jnp.where(col_j >= i, bwd, 0.0))
        w_intra.append(jnp.concatenate(blocks, axis=0).astype(BF16))

    for slot, (u_ref, y_ref, s) in enumerate(seqs):
        lanes = slice(s * n_chunks, (s + 1) * n_chunks)
        state_in = []
        for q in pair:
            x_t = u_ref[:, group_rows(q), lanes].reshape(CHUNK_COLS, n_chunks).astype(BF16)
            r = jnp.dot(w_intra[q], x_t, preferred_element_type=F32)
            y_ref[:, group_rows(q), lanes] = r.reshape(CHUNK, SSM_GROUP_CH, n_chunks)
            state_in.append(jnp.dot(ws_ref[q], x_t, preferred_element_type=F32))
        for k in range(4):
            both = jnp.concatenate([state_in[q][k * n_st:(k + 1) * n_st] for q in pair], axis=0)
            g_scr.at[k][pl.ds(slot, n_chunks, stride=SEQ_PAD), :] = both.T

    a = a_ref[0]
    shape = (SEQ_PAD, LANES)
    prf, pif, prb, pib = [jnp.broadcast_to(a[k:k + 1], shape) for k in range(4)]

    def step(c, carry):
        fr, fi, br, bi = carry
        rf = pl.ds(pl.multiple_of(c * SEQ_PAD, SEQ_PAD), SEQ_PAD)
        rb = pl.ds(pl.multiple_of((n_chunks - 1 - c) * SEQ_PAD, SEQ_PAD), SEQ_PAD)
        h_scr[0, rf, :] = fr
        h_scr[1, rf, :] = fi
        h_scr[2, rb, :] = br
        h_scr[3, rb, :] = bi
        return (prf * fr - pif * fi + g_scr[0, rf, :], prf * fi + pif * fr + g_scr[1, rf, :],
                prb * br - pib * bi + g_scr[2, rb, :], prb * bi + pib * br + g_scr[3, rb, :])

    zero = jnp.zeros(shape, F32)
    lax.fori_loop(0, n_chunks, step, (zero, zero, zero, zero))

    for slot, (u_ref, y_ref, s) in enumerate(seqs):
        lanes = slice(s * n_chunks, (s + 1) * n_chunks)
        h_ts = [h_scr.at[k][pl.ds(slot, n_chunks, stride=SEQ_PAD), :].T for k in range(4)]
        for q in pair:
            h_t = jnp.concatenate([h[q * n_st:(q + 1) * n_st] for h in h_ts], axis=0)
            x_t = u_ref[:, group_rows(q), lanes].reshape(CHUNK_COLS, n_chunks)
            y = (y_ref[:, group_rows(q), lanes].reshape(CHUNK_COLS, n_chunks)
                 + jnp.dot(cp_ref[q], h_t.astype(BF16), preferred_element_type=F32) + d_ref[q] * x_t)
            y_ref[:, group_rows(q), lanes] = y.reshape(CHUNK, SSM_GROUP_CH, n_chunks)


def _ssm(ut_p, ut_s, lag, w_state, c_pow, a_chunk, d_col, n_chunks):
    G = SSM_GROUPS // SSM_PAIR
    n_p = ut_p.shape[2] // n_chunks
    n_s = ut_s.shape[2] // n_chunks
    n_rows = n_chunks * SEQ_PAD
    act = lambda arr: pl.BlockSpec((CHUNK, SSM_PAIR * SSM_GROUP_CH, arr.shape[2]), lambda g: (0, g, 0))
    per_group = lambda arr: pl.BlockSpec(
        (arr.shape[0] // G,) + arr.shape[1:], lambda g: (g,) + (0,) * (arr.ndim - 1))
    return pl.pallas_call(
        functools.partial(_ssm_kernel, n_chunks=n_chunks, n_p=n_p, n_s=n_s),
        grid=(G,),
        in_specs=[act(ut_p), act(ut_s), per_group(lag), per_group(w_state), per_group(c_pow),
                  per_group(a_chunk), per_group(d_col)],
        out_specs=[act(ut_p), act(ut_s)],
        out_shape=[jax.ShapeDtypeStruct(ut_p.shape, F32), jax.ShapeDtypeStruct(ut_s.shape, F32)],
        scratch_shapes=[
            pltpu.VMEM((4, n_rows, LANES), F32),
            pltpu.VMEM((4, n_rows, LANES), F32),
        ],
        compiler_params=pltpu.CompilerParams(
            dimension_semantics=("arbitrary",), vmem_limit_bytes=VMEM_LIMIT),
        name="ssm_chunked",
    )(ut_p, ut_s, lag, w_state, c_pow, a_chunk, d_col)


def _post_kernel(x_ref, att_ref, yt_ref, p_ref, wglu_ref, bglu_ref, gatt_ref, gssm_ref, wout_ref,
                 gmlp_ref, w1_ref, w2_ref, gple_ref, wgate_ref, wproj_ref, gfin_ref, o_ref, ys_scr):
    k = pl.program_id(1)

    @pl.when(k == 0)
    def _():
        for l in range(SSM_WIDTH // LANES):
            for i in range(CHUNK):
                ys_scr.at[l][pl.ds(i, LANES, stride=CHUNK), :] = yt_ref[i, l * LANES:(l + 1) * LANES, :].T

    rows = pl.ds(pl.multiple_of(k * POST_TILE, POST_TILE), POST_TILE)
    ys = jnp.concatenate([ys_scr[l, rows, :] for l in range(SSM_WIDTH // LANES)], axis=1)
    g = _gelu_tanh(ys)
    ssm = g * _sigmoid(_bdot(g, wglu_ref[...]) + bglu_ref[...])
    att_n = _rms(att_ref[...], gatt_ref[...])
    ssm_n = _rms(ssm, gssm_ref[...])
    h = x_ref[...] + (_bdot(att_n, wout_ref[:ATT_WIDTH, :]) + _bdot(ssm_n, wout_ref[ATT_WIDTH:, :]))
    f = _rms(h, gmlp_ref[...]).astype(BF16)
    acc = jnp.zeros_like(h)
    for kb in range(D_FF // FF_BLK):
        cols = slice(kb * FF_BLK, (kb + 1) * FF_BLK)
        t = jnp.dot(f, w1_ref[:, cols], preferred_element_type=F32)
        t = jnp.square(jnp.maximum(t, 0.0))
        acc = acc + _bdot(t, w2_ref[cols, :])
    h = h + acc
    e = _rms(h, gple_ref[...])
    h = h + _sigmoid(_bdot(e, wgate_ref[...])) * _bdot(p_ref[...], wproj_ref[...])
    o_ref[...] = _rms(h, gfin_ref[...])


def _post(x, att, yt, p, wts):
    t = x.shape[0]
    n_in = SUPER_TILE // POST_TILE
    tile = lambda width: pl.BlockSpec((POST_TILE, width), lambda i, k: (i * n_in + k, 0))

    def resident(arr):
        return pl.BlockSpec(arr.shape, lambda i, k: (0,) * arr.ndim, pipeline_mode=pl.Buffered(1))

    return pl.pallas_call(
        _post_kernel,
        grid=(t // SUPER_TILE, n_in),
        in_specs=[tile(D_MODEL), tile(ATT_WIDTH),
                  pl.BlockSpec((CHUNK, SSM_WIDTH, LANES), lambda i, k: (0, 0, i)), tile(PLE_DIM)]
                 + [resident(w) for w in wts],
        out_specs=tile(D_MODEL),
        out_shape=jax.ShapeDtypeStruct((t, D_MODEL), F32),
        scratch_shapes=[pltpu.VMEM((SSM_WIDTH // LANES, SUPER_TILE, LANES), F32)],
        compiler_params=pltpu.CompilerParams(
            dimension_semantics=("arbitrary", "arbitrary"), vmem_limit_bytes=VMEM_LIMIT),
        name="post_mixers",
    )(x, att, yt, p, *wts)


def kernel(x_prompt, x_sample, p_prompt, p_sample, rel_bias, g_mix, w_in, ssm_a_re, ssm_a_im, ssm_log_dt, ssm_b_re, ssm_b_im, ssm_c_re, ssm_c_im, ssm_d, w_glu, b_glu, g_att_out, g_ssm_out, w_out, g_mlp, w_mlp1, w_mlp2, g_ple, w_ple_gate, w_ple_proj, g_final):
    assert g_mix.shape[0] == 1, "single-layer trunk"
    seq_len = x_prompt.shape[1]
    assert x_sample.shape[1] == seq_len and seq_len % ATT_TILE == 0
    n_p, n_s = x_prompt.shape[0], x_sample.shape[0]
    assert n_p + n_s <= SEQ_PAD
    n_chunks = seq_len // CHUNK

    w_in_bf = w_in[0].astype(BF16)
    tab = _bias_tables(rel_bias)
    ssm_wts = _ssm_weights(
        ssm_a_re[0], ssm_a_im[0], ssm_log_dt[0], ssm_b_re[0], ssm_b_im[0],
        ssm_c_re[0], ssm_c_im[0], ssm_d[0])
    row = lambda v, n: v.reshape(1, n)
    wts = (w_glu[0].astype(BF16), row(b_glu[0], SSM_WIDTH), row(g_att_out[0], ATT_WIDTH),
           row(g_ssm_out[0], SSM_WIDTH), w_out[0].astype(BF16), row(g_mlp[0], D_MODEL),
           w_mlp1[0].astype(BF16), w_mlp2[0].astype(BF16), row(g_ple[0], D_MODEL),
           w_ple_gate[0].astype(BF16), w_ple_proj[0].astype(BF16), row(g_final, D_MODEL))

    groups = []
    for x3, p4 in ((x_prompt, p_prompt), (x_sample, p_sample)):
        n = x3.shape[0]
        x = x3.reshape(n * seq_len, D_MODEL)
        qkv, ut = _inproj(x, g_mix[0], w_in_bf)
        att = _attention(qkv, tab, n, seq_len)
        groups.append((x, p4[0].reshape(n * seq_len, PLE_DIM), att, ut))
    yts = _ssm(groups[0][3], groups[1][3], *ssm_wts, n_chunks)
    outs = [_post(x, att, yt, p, wts).reshape(-1, seq_len, D_MODEL)
            for (x, p, att, _), yt in zip(groups, yts)]
    return outs[0], outs[1]
```

```python
import functools
import math

import jax
import jax.numpy as jnp
import numpy as np
from jax import lax
from jax.experimental import pallas as pl
from jax.experimental.pallas import tpu as pltpu

F32 = jnp.float32
BF16 = jnp.bfloat16

D_MODEL = 1024
ATT_HEADS = 8
HEAD_DIM = 64
ATT_WIDTH = ATT_HEADS * HEAD_DIM
SSM_WIDTH = D_MODEL - ATT_WIDTH
SSM_GROUP_CH = 16
SSM_GROUPS = SSM_WIDTH // SSM_GROUP_CH
SSM_STATE = 64
IN_COLS = 3 * ATT_WIDTH + SSM_WIDTH
D_FF = 4 * D_MODEL
PLE_DIM = 256
NUM_BUCKETS = 32
REL_MAX_DISTANCE = 1024
DIL_WINDOWS = (128, 512, 2048)
DIL_RATES = (1, 4, 16)
RMS_EPS = 1e-6
NEG_INF = -1e30
LOG2_E = math.log2(math.e)

LANES = 128
SUBLANES = 8
VMEM_LIMIT = 56 * 1024 * 1024

RADIUS = 64
Q_BLK = 128
K_BLK = Q_BLK + 2 * RADIUS
ATT_TILE = 2048
PHASES = 4
HEADS_PER_STEP = LANES // HEAD_DIM
ATT_GROUP = 16
N_VARIANTS = 3

CHUNK = 16
CHUNK_COLS = CHUNK * SSM_GROUP_CH
SEQ_PAD = SUBLANES
SSM_PAIR = LANES // SSM_STATE

SUPER_TILE = CHUNK * LANES
IN_TILE = 1024
POST_TILE = 512
FF_BLK = 1024


def _rms(x, g):
    return x * lax.rsqrt(jnp.mean(x * x, axis=-1, keepdims=True) + RMS_EPS) * g


def _sigmoid(x):
    return 1.0 / (1.0 + jnp.exp(-x))


def _gelu_tanh(x):
    c = math.sqrt(2.0 / math.pi)
    return 0.5 * x * (1.0 + jnp.tanh(c * (x + 0.044715 * (x * x * x))))


def _bdot(a, b):
    return jnp.dot(a.astype(BF16), b.astype(BF16), preferred_element_type=F32)


def _inproj_kernel(x_ref, g_ref, w_ref, qkv_ref, ut_ref, u_scr):
    k = pl.program_id(1)
    a = _rms(x_ref[...], g_ref[...])
    z = _bdot(a, w_ref[...])
    qkv_ref[...] = z[:, :3 * ATT_WIDTH]
    rows = pl.ds(pl.multiple_of(k * IN_TILE, IN_TILE), IN_TILE)
    for l in range(SSM_WIDTH // LANES):
        u_scr[l, rows, :] = z[:, 3 * ATT_WIDTH + l * LANES:3 * ATT_WIDTH + (l + 1) * LANES]

    @pl.when(k == SUPER_TILE // IN_TILE - 1)
    def _():
        for l in range(SSM_WIDTH // LANES):
            for j in range(CHUNK):
                ut_ref[j, l * LANES:(l + 1) * LANES, :] = u_scr.at[l][pl.ds(j, LANES, stride=CHUNK), :].T


def _inproj(x, g_mix, w_in_bf):
    t = x.shape[0]
    n_in = SUPER_TILE // IN_TILE
    return pl.pallas_call(
        _inproj_kernel,
        grid=(t // SUPER_TILE, n_in),
        in_specs=[
            pl.BlockSpec((IN_TILE, D_MODEL), lambda i, k: (i * n_in + k, 0)),
            pl.BlockSpec((1, D_MODEL), lambda i, k: (0, 0)),
            pl.BlockSpec((D_MODEL, IN_COLS), lambda i, k: (0, 0)),
        ],
        out_specs=[
            pl.BlockSpec((IN_TILE, 3 * ATT_WIDTH), lambda i, k: (i * n_in + k, 0)),
            pl.BlockSpec((CHUNK, SSM_WIDTH, LANES), lambda i, k: (0, 0, i)),
        ],
        out_shape=[
            jax.ShapeDtypeStruct((t, 3 * ATT_WIDTH), F32),
            jax.ShapeDtypeStruct((CHUNK, SSM_WIDTH, t // CHUNK), F32),
        ],
        scratch_shapes=[pltpu.VMEM((SSM_WIDTH // LANES, SUPER_TILE, LANES), F32)],
        compiler_params=pltpu.CompilerParams(
            dimension_semantics=("arbitrary", "arbitrary"), vmem_limit_bytes=VMEM_LIMIT),
        name="inproj",
    )(x, g_mix.reshape(1, D_MODEL), w_in_bf)


def _t5_bucket_np(rel):
    half = NUM_BUCKETS // 2
    n = -rel
    ret = np.where(n < 0, half, 0)
    n = np.abs(n)
    max_exact = half // 2
    nf = np.maximum(n, 1).astype(np.float64)
    large = max_exact + (np.log(nf / max_exact) / math.log(REL_MAX_DISTANCE / max_exact)
                         * (half - max_exact)).astype(np.int64)
    large = np.minimum(large, half - 1)
    return ret + np.where(n < max_exact, n, large)


def _bucket_tables():
    out = np.zeros((len(DIL_RATES), N_VARIANTS, Q_BLK, K_BLK), np.int32)
    for b, d in enumerate(DIL_RATES):
        qi = np.arange(Q_BLK)
        ci = np.arange(K_BLK)
        if d == 1:
            qi = (qi % (Q_BLK // PHASES)) * PHASES + qi // (Q_BLK // PHASES)
            ci = (ci % (K_BLK // PHASES)) * PHASES + ci // (K_BLK // PHASES)
        for v, shift in enumerate((0, -RADIUS, -2 * RADIUS)):
            off = ci[None, :] - qi[:, None] + shift
            bk = _t5_bucket_np(off * d)
            out[b, v] = np.where(np.abs(off) <= RADIUS, bk, NUM_BUCKETS)
    return out.reshape(len(DIL_RATES) * N_VARIANTS, Q_BLK, K_BLK)


def _bias_kernel(rel_ref, bk_ref, tab_ref):
    hp = pl.program_id(0)
    n_bv = bk_ref.shape[0]
    for bv in range(n_bv):
        bk = bk_ref[bv]
        for h2 in range(HEADS_PER_STEP):
            acc = jnp.full(bk.shape, NEG_INF, F32)
            for b in range(NUM_BUCKETS):
                acc = jnp.where(bk == b, rel_ref[b, hp * HEADS_PER_STEP + h2] * LOG2_E, acc)
            tab_ref[0, bv * HEADS_PER_STEP + h2] = acc


def _bias_tables(rel_bias):
    bk = jnp.asarray(_bucket_tables())
    n_bv = bk.shape[0]
    n_hp = ATT_HEADS // HEADS_PER_STEP
    return pl.pallas_call(
        _bias_kernel,
        grid=(n_hp,),
        in_specs=[
            pl.BlockSpec(memory_space=pltpu.SMEM),
            pl.BlockSpec((n_bv, Q_BLK, K_BLK), lambda h: (0, 0, 0)),
        ],
        out_specs=pl.BlockSpec((1, n_bv * HEADS_PER_STEP, Q_BLK, K_BLK), lambda h: (h, 0, 0, 0)),
        out_shape=jax.ShapeDtypeStruct((n_hp, n_bv * HEADS_PER_STEP, Q_BLK, K_BLK), F32),
        compiler_params=pltpu.CompilerParams(dimension_semantics=("arbitrary",)),
        name="bias_tables",
    )(rel_bias, bk)


def _att_kernel(q_ref, k_ref, v_ref, tab_ref, o_ref, k4_scr, va_scr, vb_scr, qa_scr, qb_scr,
                acc_scr, m_scr, den_scr, s_scr, p_scr, *, seq_len):
    t = pl.program_id(2)
    n4 = seq_len // PHASES
    t4 = ATT_TILE // PHASES
    n_sub = ATT_TILE // Q_BLK
    stage = 256

    def head0_mask(rows):
        return lax.broadcasted_iota(jnp.int32, (rows, LANES), 1) < HEAD_DIM

    head0 = head0_mask(Q_BLK)
    head0_s = head0_mask(stage)

    @pl.when(t == 0)
    def _():
        def body(c, carry):
            for r in range(PHASES):
                src = pl.ds(r + PHASES * stage * c, stage, stride=PHASES)
                dst = pl.ds(pl.multiple_of(stage * c, stage), stage)
                k4_scr[r, dst, :] = k_ref[src, :]
                v = v_ref[src, :]
                va_scr[r, dst, :] = jnp.where(head0_s, v, 1.0)
                vb_scr[r, dst, :] = jnp.where(head0_s, 1.0, v)
            return carry
        lax.fori_loop(0, n4 // stage, body, 0)

    for r in range(PHASES):
        for c in range(t4 // stage):
            q = q_ref[pl.ds(r + PHASES * stage * c, stage, stride=PHASES), :] * (LOG2_E * HEAD_DIM ** -0.5)
            qa_scr[r, c * stage:(c + 1) * stage, :] = jnp.where(head0_s, q, 0.0)
            qb_scr[r, c * stage:(c + 1) * stage, :] = jnp.where(head0_s, 0.0, q)

    def pieces(scr, start, n):
        return jnp.concatenate([scr[r, pl.ds(start, n), :] for r in range(PHASES)], axis=0)

    def sub_block(b, idx):
        d = DIL_RATES[b]
        n_m = seq_len // d
        if d == 1:
            m0 = t * ATT_TILE + Q_BLK * idx
        elif d == PHASES:
            r, blk = idx % PHASES, idx // PHASES
            m0 = t * t4 + Q_BLK * blk
        else:
            r, r16 = idx % PHASES, idx // PHASES
            m0 = t * (ATT_TILE // d)
        k_start = jnp.clip(m0 - RADIUS, 0, n_m - K_BLK)
        variant = jnp.where(m0 < RADIUS, 0, jnp.where(m0 > n_m - Q_BLK - RADIUS, 2, 1))
        if d == 1:
            q0 = (Q_BLK // PHASES) * idx
            k0 = pl.multiple_of(k_start // PHASES, SUBLANES)
            load_q = lambda scr: pieces(scr, q0, Q_BLK // PHASES)
            load_k = lambda scr: pieces(scr, k0, K_BLK // PHASES)

            def store(scr, val):
                n = Q_BLK // PHASES
                for rr in range(PHASES):
                    scr[b, rr, q0:q0 + n, :] = val[rr * n:(rr + 1) * n]
        elif d == PHASES:
            load_q = lambda scr: scr[r, blk * Q_BLK:(blk + 1) * Q_BLK, :]
            load_k = lambda scr: scr[r, pl.ds(pl.multiple_of(k_start, SUBLANES), K_BLK), :]

            def store(scr, val):
                scr[b, r, blk * Q_BLK:(blk + 1) * Q_BLK, :] = val
        else:
            e = d // PHASES
            load_q = lambda scr: scr.at[r][pl.ds(r16, Q_BLK, stride=e), :]
            load_k = lambda scr: scr.at[r][pl.ds(r16 + e * k_start, K_BLK, stride=e), :]

            def store(scr, val):
                scr.at[b, r][pl.ds(r16, Q_BLK, stride=e), :] = val
        return load_q, load_k, store, variant

    for b in range(len(DIL_RATES)):
        for g0 in range(0, n_sub, ATT_GROUP):
            stores, key_loaders = [], []
            for j in range(ATT_GROUP):
                load_q, load_k, store, variant = sub_block(b, g0 + j)
                stores.append(store)
                key_loaders.append(load_k)
                kb = load_k(k4_scr).astype(BF16)
                for h2, q_scr in enumerate((qa_scr, qb_scr)):
                    s = lax.dot_general(load_q(q_scr).astype(BF16), kb, (((1,), (1,)), ((), ())),
                                        preferred_element_type=F32)
                    s_scr[j * HEADS_PER_STEP + h2] = (
                        s + tab_ref[0, (b * N_VARIANTS + variant) * HEADS_PER_STEP + h2])
            for j in range(ATT_GROUP):
                ms = []
                for h2 in range(HEADS_PER_STEP):
                    s = s_scr[j * HEADS_PER_STEP + h2]
                    m = jnp.max(s, axis=-1, keepdims=True)
                    p_scr[j * HEADS_PER_STEP + h2] = jnp.exp2((s - m).astype(BF16))
                    ms.append(m)
                stores[j](m_scr, jnp.where(head0, ms[0], ms[1]))
            for j in range(ATT_GROUP):
                outs = [jnp.dot(p_scr[j * HEADS_PER_STEP + h2], key_loaders[j](vh_scr).astype(BF16),
                                preferred_element_type=F32)
                        for h2, vh_scr in enumerate((va_scr, vb_scr))]
                stores[j](acc_scr, jnp.where(head0, outs[0], outs[1]))
                stores[j](den_scr, jnp.where(head0, outs[1], outs[0]))

    for r in range(PHASES):
        m_all = m_scr[:, r]
        m = jnp.max(m_all, axis=0)
        num = jnp.zeros((t4, LANES), F32)
        den = jnp.zeros((t4, LANES), F32)
        for b in range(len(DIL_RATES)):
            w = jnp.exp2(m_all[b] - m)
            num = num + w * acc_scr[b, r]
            den = den + w * pltpu.roll(den_scr[b, r], HEAD_DIM, axis=1)
        o_ref[pl.ds(r, t4, stride=PHASES), :] = num / den


def _attention(qkv, tab, n_seq, seq_len):
    n_hp = ATT_HEADS // HEADS_PER_STEP
    n_t = seq_len // ATT_TILE
    n_br = len(DIL_RATES)
    return pl.pallas_call(
        functools.partial(_att_kernel, seq_len=seq_len),
        grid=(n_seq, n_hp, n_t),
        in_specs=[
            pl.BlockSpec((ATT_TILE, LANES), lambda s, h, t: (s * n_t + t, h)),
            pl.BlockSpec((seq_len, LANES), lambda s, h, t: (s, n_hp + h)),
            pl.BlockSpec((seq_len, LANES), lambda s, h, t: (s, 2 * n_hp + h)),
            pl.BlockSpec((1,) + tab.shape[1:], lambda s, h, t: (h, 0, 0, 0)),
        ],
        out_specs=pl.BlockSpec((ATT_TILE, LANES), lambda s, h, t: (s * n_t + t, h)),
        out_shape=jax.ShapeDtypeStruct((n_seq * seq_len, ATT_WIDTH), F32),
        scratch_shapes=[
            pltpu.VMEM((PHASES, seq_len // PHASES, LANES), F32),
            pltpu.VMEM((PHASES, seq_len // PHASES, LANES), F32),
            pltpu.VMEM((PHASES, seq_len // PHASES, LANES), F32),
            pltpu.VMEM((PHASES, ATT_TILE // PHASES, LANES), F32),
            pltpu.VMEM((PHASES, ATT_TILE // PHASES, LANES), F32),
            pltpu.VMEM((n_br, PHASES, ATT_TILE // PHASES, LANES), F32),
            pltpu.VMEM((n_br, PHASES, ATT_TILE // PHASES, LANES), F32),
            pltpu.VMEM((n_br, PHASES, ATT_TILE // PHASES, LANES), F32),
            pltpu.VMEM((ATT_GROUP * HEADS_PER_STEP, Q_BLK, K_BLK), F32),
            pltpu.VMEM((ATT_GROUP * HEADS_PER_STEP, Q_BLK, K_BLK), BF16),
        ],
        compiler_params=pltpu.CompilerParams(
            dimension_semantics=("arbitrary", "arbitrary", "arbitrary"),
            vmem_limit_bytes=VMEM_LIMIT),
        name="dilated_attention",
    )(qkv, qkv, qkv, tab)


def _ssm_weights(a_re, a_im, log_dt, b_re, b_im, c_re, c_im, d_skip):
    hi = lax.Precision.HIGHEST
    G, N, HC = SSM_GROUPS, SSM_STATE, SSM_GROUP_CH
    dt = jnp.exp(log_dt)[..., None]
    mag = jnp.exp(a_re * dt)
    ab_re = mag * jnp.cos(a_im * dt)
    ab_im = mag * jnp.sin(a_im * dt)
    inv = 1.0 / (a_re * a_re + a_im * a_im)
    f_re = ((ab_re - 1.0) * a_re + ab_im * a_im) * inv
    f_im = (ab_im * a_re - (ab_re - 1.0) * a_im) * inv
    bb_re = f_re[..., None] * b_re - f_im[..., None] * b_im
    bb_im = f_re[..., None] * b_im + f_im[..., None] * b_re
    tau = jnp.arange(CHUNK + 1, dtype=F32)
    mag_t = jnp.exp((a_re * dt)[..., None] * tau)
    pr = mag_t * jnp.cos((a_im * dt)[..., None] * tau)
    pi = mag_t * jnp.sin((a_im * dt)[..., None] * tau)

    def a_pow_b(direction, rev):
        order = slice(CHUNK - 1, None, -1) if rev else slice(0, CHUNK)
        pr_c = jnp.repeat(pr[direction][..., order], HC, axis=-1)
        pi_c = jnp.repeat(pi[direction][..., order], HC, axis=-1)
        br = jnp.tile(bb_re[direction], (1, 1, CHUNK))
        bi = jnp.tile(bb_im[direction], (1, 1, CHUNK))
        return pr_c * br - pi_c * bi, pr_c * bi + pi_c * br

    def lag_table(direction, ab):
        return (jnp.einsum('gcn,gnx->gcx', c_re[direction], ab[0], precision=hi)
                - jnp.einsum('gcn,gnx->gcx', c_im[direction], ab[1], precision=hi))

    ab_f = a_pow_b(0, True)
    ab_b = a_pow_b(1, False)
    lag = jnp.stack([lag_table(0, ab_f), lag_table(1, ab_b)], axis=1)
    w_state = jnp.concatenate([ab_f[0], ab_f[1], ab_b[0], ab_b[1]], axis=1).astype(BF16)
    pr = jnp.moveaxis(pr, -1, 0)
    pi = jnp.moveaxis(pi, -1, 0)

    def state_out(direction, pr_sel, pi_sel):
        cr = c_re[direction][:, None, :, :]
        ci = c_im[direction][:, None, :, :]
        pr_s = pr_sel.transpose(1, 0, 2)[:, :, None, :]
        pi_s = pi_sel.transpose(1, 0, 2)[:, :, None, :]
        from_re = (cr * pr_s - ci * pi_s).reshape(G, CHUNK_COLS, N)
        from_im = (-cr * pi_s - ci * pr_s).reshape(G, CHUNK_COLS, N)
        return jnp.concatenate([from_re, from_im], axis=2)

    c_pow = jnp.concatenate(
        [state_out(0, pr[1:CHUNK + 1, 0], pi[1:CHUNK + 1, 0]),
         state_out(1, pr[CHUNK:0:-1, 1], pi[CHUNK:0:-1, 1])], axis=2).astype(BF16)

    a_chunk = jnp.stack([pr[CHUNK, 0], pi[CHUNK, 0], pr[CHUNK, 1], pi[CHUNK, 1]], axis=1)
    a_chunk = a_chunk.reshape(G // SSM_PAIR, SSM_PAIR, 4, N).transpose(0, 2, 1, 3).reshape(
        G // SSM_PAIR, 4, SSM_PAIR * N)
    d_col = jnp.tile(d_skip[:, None, :], (1, CHUNK, 1)).reshape(G, CHUNK_COLS, 1)
    return lag, w_state, c_pow, a_chunk, d_col


def _ssm_kernel(utp_ref, uts_ref, lag_ref, ws_ref, cp_ref, a_ref, d_ref, ytp_ref, yts_ref, g_scr, h_scr,
                *, n_chunks, n_p, n_s):
    seqs = [(utp_ref, ytp_ref, s) for s in range(n_p)] + [(uts_ref, yts_ref, s) for s in range(n_s)]
    n_st = SSM_STATE
    pair = range(SSM_PAIR)

    def group_rows(q):
        return slice(q * SSM_GROUP_CH, (q + 1) * SSM_GROUP_CH)

    @pl.when(pl.program_id(0) == 0)
    def _():
        g_scr[...] = jnp.zeros(g_scr.shape, F32)

    col_j = lax.broadcasted_iota(jnp.int32, (SSM_GROUP_CH, CHUNK_COLS), 1) // SSM_GROUP_CH
    w_intra = []
    for q in pair:
        lag_f, lag_b = lag_ref[q, 0], lag_ref[q, 1]
        blocks = []
        for i in range(CHUNK):
            fwd = pltpu.roll(lag_f, (i + 1) * SSM_GROUP_CH % CHUNK_COLS, axis=1)
            bwd = pltpu.roll(lag_b, i * SSM_GROUP_CH, axis=1)
            blocks.append(jnp.where(col_j <= i, fwd, 0.0) + jnp.where(col_j >= i, bwd, 0.0))
        w_intra.append(jnp.concatenate(blocks, axis=0).astype(BF16))

    for slot, (u_ref, y_ref, s) in enumerate(seqs):
        lanes = slice(s * n_chunks, (s + 1) * n_chunks)
        state_in = []
        for q in pair:
            x_t = u_ref[:, group_rows(q), lanes].reshape(CHUNK_COLS, n_chunks).astype(BF16)
            r = jnp.dot(w_intra[q], x_t, preferred_element_type=F32)
            y_ref[:, group_rows(q), lanes] = r.reshape(CHUNK, SSM_GROUP_CH, n_chunks)
            state_in.append(jnp.dot(ws_ref[q], x_t, preferred_element_type=F32))
        for k in range(4):
            both = jnp.concatenate([state_in[q][k * n_st:(k + 1) * n_st] for q in pair], axis=0)
            g_scr.at[k][pl.ds(slot, n_chunks, stride=SEQ_PAD), :] = both.T

    a = a_ref[0]
    shape = (SEQ_PAD, LANES)
    prf, pif, prb, pib = [jnp.broadcast_to(a[k:k + 1], shape) for k in range(4)]

    def step(c, carry):
        fr, fi, br, bi = carry
        rf = pl.ds(pl.multiple_of(c * SEQ_PAD, SEQ_PAD), SEQ_PAD)
        rb = pl.ds(pl.multiple_of((n_chunks - 1 - c) * SEQ_PAD, SEQ_PAD), SEQ_PAD)
        h_scr[0, rf, :] = fr
        h_scr[1, rf, :] = fi
        h_scr[2, rb, :] = br
        h_scr[3, rb, :] = bi
        return (prf * fr - pif * fi + g_scr[0, rf, :], prf * fi + pif * fr + g_scr[1, rf, :],
                prb * br - pib * bi + g_scr[2, rb, :], prb * bi + pib * br + g_scr[3, rb, :])

    zero = jnp.zeros(shape, F32)
    lax.fori_loop(0, n_chunks, step, (zero, zero, zero, zero))

    for slot, (u_ref, y_ref, s) in enumerate(seqs):
        lanes = slice(s * n_chunks, (s + 1) * n_chunks)
        h_ts = [h_scr.at[k][pl.ds(slot, n_chunks, stride=SEQ_PAD), :].T for k in range(4)]
        for q in pair:
            h_t = jnp.concatenate([h[q * n_st:(q + 1) * n_st] for h in h_ts], axis=0)
            x_t = u_ref[:, group_rows(q), lanes].reshape(CHUNK_COLS, n_chunks)
            y = (y_ref[:, group_rows(q), lanes].reshape(CHUNK_COLS, n_chunks)
                 + jnp.dot(cp_ref[q], h_t.astype(BF16), preferred_element_type=F32) + d_ref[q] * x_t)
            y_ref[:, group_rows(q), lanes] = y.reshape(CHUNK, SSM_GROUP_CH, n_chunks)


def _ssm(ut_p, ut_s, lag, w_state, c_pow, a_chunk, d_col, n_chunks):
    G = SSM_GROUPS // SSM_PAIR
    n_p = ut_p.shape[2] // n_chunks
    n_s = ut_s.shape[2] // n_chunks
    n_rows = n_chunks * SEQ_PAD
    act = lambda arr: pl.BlockSpec((CHUNK, SSM_PAIR * SSM_GROUP_CH, arr.shape[2]), lambda g: (0, g, 0))
    per_group = lambda arr: pl.BlockSpec(
        (arr.shape[0] // G,) + arr.shape[1:], lambda g: (g,) + (0,) * (arr.ndim - 1))
    return pl.pallas_call(
        functools.partial(_ssm_kernel, n_chunks=n_chunks, n_p=n_p, n_s=n_s),
        grid=(G,),
        in_specs=[act(ut_p), act(ut_s), per_group(lag), per_group(w_state), per_group(c_pow),
                  per_group(a_chunk), per_group(d_col)],
        out_specs=[act(ut_p), act(ut_s)],
        out_shape=[jax.ShapeDtypeStruct(ut_p.shape, F32), jax.ShapeDtypeStruct(ut_s.shape, F32)],
        scratch_shapes=[
            pltpu.VMEM((4, n_rows, LANES), F32),
            pltpu.VMEM((4, n_rows, LANES), F32),
        ],
        compiler_params=pltpu.CompilerParams(
            dimension_semantics=("arbitrary",), vmem_limit_bytes=VMEM_LIMIT),
        name="ssm_chunked",
    )(ut_p, ut_s, lag, w_state, c_pow, a_chunk, d_col)


def _post_kernel(x_ref, att_ref, yt_ref, p_ref, wglu_ref, bglu_ref, gatt_ref, gssm_ref, wout_ref,
                 gmlp_ref, w1_ref, w2_ref, gple_ref, wgate_ref, wproj_ref, gfin_ref, o_ref, ys_scr):
    k = pl.program_id(1)

    @pl.when(k == 0)
    def _():
        for l in range(SSM_WIDTH // LANES):
            for i in range(CHUNK):
                ys_scr.at[l][pl.ds(i, LANES, stride=CHUNK), :] = yt_ref[i, l * LANES:(l + 1) * LANES, :].T

    rows = pl.ds(pl.multiple_of(k * POST_TILE, POST_TILE), POST_TILE)
    ys = jnp.concatenate([ys_scr[l, rows, :] for l in range(SSM_WIDTH // LANES)], axis=1)
    g = _gelu_tanh(ys)
    ssm = g * _sigmoid(_bdot(g, wglu_ref[...]) + bglu_ref[...])
    att_n = _rms(att_ref[...], gatt_ref[...])
    ssm_n = _rms(ssm, gssm_ref[...])
    h = x_ref[...] + (_bdot(att_n, wout_ref[:ATT_WIDTH, :]) + _bdot(ssm_n, wout_ref[ATT_WIDTH:, :]))
    f = _rms(h, gmlp_ref[...]).astype(BF16)
    acc = jnp.zeros_like(h)
    for kb in range(D_FF // FF_BLK):
        cols = slice(kb * FF_BLK, (kb + 1) * FF_BLK)
        t = jnp.dot(f, w1_ref[:, cols], preferred_element_type=F32)
        t = jnp.square(jnp.maximum(t, 0.0))
        acc = acc + _bdot(t, w2_ref[cols, :])
    h = h + acc
    e = _rms(h, gple_ref[...])
    h = h + _sigmoid(_bdot(e, wgate_ref[...])) * _bdot(p_ref[...], wproj_ref[...])
    o_ref[...] = _rms(h, gfin_ref[...])


def _post(x, att, yt, p, wts):
    t = x.shape[0]
    n_in = SUPER_TILE // POST_TILE
    tile = lambda width: pl.BlockSpec((POST_TILE, width), lambda i, k: (i * n_in + k, 0))

    def resident(arr):
        return pl.BlockSpec(arr.shape, lambda i, k: (0,) * arr.ndim, pipeline_mode=pl.Buffered(1))

    return pl.pallas_call(
        _post_kernel,
        grid=(t // SUPER_TILE, n_in),
        in_specs=[tile(D_MODEL), tile(ATT_WIDTH),
                  pl.BlockSpec((CHUNK, SSM_WIDTH, LANES), lambda i, k: (0, 0, i)), tile(PLE_DIM)]
                 + [resident(w) for w in wts],
        out_specs=tile(D_MODEL),
        out_shape=jax.ShapeDtypeStruct((t, D_MODEL), F32),
        scratch_shapes=[pltpu.VMEM((SSM_WIDTH // LANES, SUPER_TILE, LANES), F32)],
        compiler_params=pltpu.CompilerParams(
            dimension_semantics=("arbitrary", "arbitrary"), vmem_limit_bytes=VMEM_LIMIT),
        name="post_mixers",
    )(x, att, yt, p, *wts)


def kernel(x_prompt, x_sample, p_prompt, p_sample, rel_bias, g_mix, w_in, ssm_a_re, ssm_a_im, ssm_log_dt, ssm_b_re, ssm_b_im, ssm_c_re, ssm_c_im, ssm_d, w_glu, b_glu, g_att_out, g_ssm_out, w_out, g_mlp, w_mlp1, w_mlp2, g_ple, w_ple_gate, w_ple_proj, g_final):
    assert g_mix.shape[0] == 1, "single-layer trunk"
    seq_len = x_prompt.shape[1]
    assert x_sample.shape[1] == seq_len and seq_len % ATT_TILE == 0
    n_p, n_s = x_prompt.shape[0], x_sample.shape[0]
    assert n_p + n_s <= SEQ_PAD
    n_chunks = seq_len // CHUNK

    w_in_bf = w_in[0].astype(BF16)
    tab = _bias_tables(rel_bias)
    ssm_wts = _ssm_weights(
        ssm_a_re[0], ssm_a_im[0], ssm_log_dt[0], ssm_b_re[0], ssm_b_im[0],
        ssm_c_re[0], ssm_c_im[0], ssm_d[0])
    row = lambda v, n: v.reshape(1, n)
    wts = (w_glu[0].astype(BF16), row(b_glu[0], SSM_WIDTH), row(g_att_out[0], ATT_WIDTH),
           row(g_ssm_out[0], SSM_WIDTH), w_out[0].astype(BF16), row(g_mlp[0], D_MODEL),
           w_mlp1[0].astype(BF16), w_mlp2[0].astype(BF16), row(g_ple[0], D_MODEL),
           w_ple_gate[0].astype(BF16), w_ple_proj[0].astype(BF16), row(g_final, D_MODEL))

    groups = []
    for x3, p4 in ((x_prompt, p_prompt), (x_sample, p_sample)):
        n = x3.shape[0]
        x = x3.reshape(n * seq_len, D_MODEL)
        qkv, ut = _inproj(x, g_mix[0], w_in_bf)
        att = _attention(qkv, tab, n, seq_len)
        groups.append((x, p4[0].reshape(n * seq_len, PLE_DIM), att, ut))
    yts = _ssm(groups[0][3], groups[1][3], *ssm_wts, n_chunks)
    outs = [_post(x, att, yt, p, wts).reshape(-1, seq_len, D_MODEL)
            for (x, p, att, _), yt in zip(groups, yts)]
    return outs[0], outs[1]
```

```python
import functools
import math

import jax
import jax.numpy as jnp
import numpy as np
from jax import lax
from jax.experimental import pallas as pl
from jax.experimental.pallas import tpu as pltpu

F32 = jnp.float32
BF16 = jnp.bfloat16

D_MODEL = 1024
ATT_HEADS = 8
HEAD_DIM = 64
ATT_WIDTH = ATT_HEADS * HEAD_DIM
SSM_WIDTH = D_MODEL - ATT_WIDTH
SSM_GROUP_CH = 16
SSM_GROUPS = SSM_WIDTH // SSM_GROUP_CH
SSM_STATE = 64
IN_COLS = 3 * ATT_WIDTH + SSM_WIDTH
D_FF = 4 * D_MODEL
PLE_DIM = 256
NUM_BUCKETS = 32
REL_MAX_DISTANCE = 1024
DIL_WINDOWS = (128, 512, 2048)
DIL_RATES = (1, 4, 16)
RMS_EPS = 1e-6
NEG_INF = -1e30
LOG2_E = math.log2(math.e)

LANES = 128
SUBLANES = 8
VMEM_LIMIT = 56 * 1024 * 1024

RADIUS = 64
Q_BLK = 128
K_BLK = Q_BLK + 2 * RADIUS
ATT_TILE = 2048
PHASES = 4
HEADS_PER_STEP = LANES // HEAD_DIM
ATT_GROUP = 16
N_VARIANTS = 3

CHUNK = 16
CHUNK_COLS = CHUNK * SSM_GROUP_CH
SEQ_PAD = SUBLANES
SSM_PAIR = LANES // SSM_STATE
SCAN_UNROLL = 8

SUPER_TILE = CHUNK * LANES
IN_TILE = 1024
POST_TILE = 512
FF_BLK = 1024


def _rms(x, g):
    return x * lax.rsqrt(jnp.mean(x * x, axis=-1, keepdims=True) + RMS_EPS) * g


def _sigmoid(x):
    return 1.0 / (1.0 + jnp.exp(-x))


def _gelu_tanh(x):
    c = math.sqrt(2.0 / math.pi)
    return 0.5 * x * (1.0 + jnp.tanh(c * (x + 0.044715 * (x * x * x))))


def _bdot(a, b):
    return jnp.dot(a.astype(BF16), b.astype(BF16), preferred_element_type=F32)


def _inproj_kernel(x_ref, g_ref, w_ref, qkv_ref, ut_ref, u_scr):
    k = pl.program_id(1)
    a = _rms(x_ref[...], g_ref[...])
    z = _bdot(a, w_ref[...])
    qkv_ref[...] = z[:, :3 * ATT_WIDTH]
    rows = pl.ds(pl.multiple_of(k * IN_TILE, IN_TILE), IN_TILE)
    for l in range(SSM_WIDTH // LANES):
        u_scr[l, rows, :] = z[:, 3 * ATT_WIDTH + l * LANES:3 * ATT_WIDTH + (l + 1) * LANES]

    @pl.when(k == SUPER_TILE // IN_TILE - 1)
    def _():
        for l in range(SSM_WIDTH // LANES):
            for j in range(CHUNK):
                ut_ref[j, l * LANES:(l + 1) * LANES, :] = u_scr.at[l][pl.ds(j, LANES, stride=CHUNK), :].T


def _inproj(x, g_mix, w_in_bf):
    t = x.shape[0]
    n_in = SUPER_TILE // IN_TILE
    return pl.pallas_call(
        _inproj_kernel,
        grid=(t // SUPER_TILE, n_in),
        in_specs=[
            pl.BlockSpec((IN_TILE, D_MODEL), lambda i, k: (i * n_in + k, 0)),
            pl.BlockSpec((1, D_MODEL), lambda i, k: (0, 0)),
            pl.BlockSpec((D_MODEL, IN_COLS), lambda i, k: (0, 0)),
        ],
        out_specs=[
            pl.BlockSpec((IN_TILE, 3 * ATT_WIDTH), lambda i, k: (i * n_in + k, 0)),
            pl.BlockSpec((CHUNK, SSM_WIDTH, LANES), lambda i, k: (0, 0, i)),
        ],
        out_shape=[
            jax.ShapeDtypeStruct((t, 3 * ATT_WIDTH), F32),
            jax.ShapeDtypeStruct((CHUNK, SSM_WIDTH, t // CHUNK), F32),
        ],
        scratch_shapes=[pltpu.VMEM((SSM_WIDTH // LANES, SUPER_TILE, LANES), F32)],
        compiler_params=pltpu.CompilerParams(
            dimension_semantics=("arbitrary", "arbitrary"), vmem_limit_bytes=VMEM_LIMIT),
        name="inproj",
    )(x, g_mix.reshape(1, D_MODEL), w_in_bf)


def _t5_bucket_np(rel):
    half = NUM_BUCKETS // 2
    n = -rel
    ret = np.where(n < 0, half, 0)
    n = np.abs(n)
    max_exact = half // 2
    nf = np.maximum(n, 1).astype(np.float64)
    large = max_exact + (np.log(nf / max_exact) / math.log(REL_MAX_DISTANCE / max_exact)
                         * (half - max_exact)).astype(np.int64)
    large = np.minimum(large, half - 1)
    return ret + np.where(n < max_exact, n, large)


def _bucket_tables():
    out = np.zeros((len(DIL_RATES), N_VARIANTS, Q_BLK, K_BLK), np.int32)
    for b, d in enumerate(DIL_RATES):
        qi = np.arange(Q_BLK)
        ci = np.arange(K_BLK)
        if d == 1:
            qi = (qi % (Q_BLK // PHASES)) * PHASES + qi // (Q_BLK // PHASES)
            ci = (ci % (K_BLK // PHASES)) * PHASES + ci // (K_BLK // PHASES)
        for v, shift in enumerate((0, -RADIUS, -2 * RADIUS)):
            off = ci[None, :] - qi[:, None] + shift
            bk = _t5_bucket_np(off * d)
            out[b, v] = np.where(np.abs(off) <= RADIUS, bk, NUM_BUCKETS)
    return out.reshape(len(DIL_RATES) * N_VARIANTS, Q_BLK, K_BLK)


def _bias_kernel(rel_ref, bk_ref, tab_ref, *, buckets):
    hp = pl.program_id(0)
    for bv in range(buckets.shape[0]):
        for r0 in range(0, Q_BLK, SUBLANES):
            for c0 in range(0, K_BLK, LANES):
                rows, cols = slice(r0, r0 + SUBLANES), slice(c0, c0 + LANES)
                bk = bk_ref[bv, rows, cols]
                present = [int(b) for b in np.unique(buckets[bv, rows, cols]) if b < NUM_BUCKETS]
                hits = [bk == b for b in present]
                for h2 in range(HEADS_PER_STEP):
                    acc = jnp.full(bk.shape, NEG_INF, F32)
                    for b, hit in zip(present, hits):
                        acc = jnp.where(hit, rel_ref[b, hp * HEADS_PER_STEP + h2] * LOG2_E, acc)
                    tab_ref[0, bv * HEADS_PER_STEP + h2, rows, cols] = acc


def _bias_tables(rel_bias):
    buckets = _bucket_tables()
    bk = jnp.asarray(buckets)
    n_bv = bk.shape[0]
    n_hp = ATT_HEADS // HEADS_PER_STEP
    return pl.pallas_call(
        functools.partial(_bias_kernel, buckets=buckets),
        grid=(n_hp,),
        in_specs=[
            pl.BlockSpec(memory_space=pltpu.SMEM),
            pl.BlockSpec((n_bv, Q_BLK, K_BLK), lambda h: (0, 0, 0)),
        ],
        out_specs=pl.BlockSpec((1, n_bv * HEADS_PER_STEP, Q_BLK, K_BLK), lambda h: (h, 0, 0, 0)),
        out_shape=jax.ShapeDtypeStruct((n_hp, n_bv * HEADS_PER_STEP, Q_BLK, K_BLK), F32),
        compiler_params=pltpu.CompilerParams(dimension_semantics=("arbitrary",)),
        name="bias_tables",
    )(rel_bias, bk)


def _att_kernel(q_ref, k_ref, v_ref, tab_ref, o_ref, k4_scr, va_scr, vb_scr, qa_scr, qb_scr,
                acc_scr, m_scr, den_scr, s_scr, p_scr, *, seq_len):
    t = pl.program_id(2)
    n4 = seq_len // PHASES
    t4 = ATT_TILE // PHASES
    n_sub = ATT_TILE // Q_BLK
    stage = 256

    def head0_mask(rows):
        return lax.broadcasted_iota(jnp.int32, (rows, LANES), 1) < HEAD_DIM

    head0 = head0_mask(Q_BLK)
    head0_s = head0_mask(stage)

    @pl.when(t == 0)
    def _():
        def body(c, carry):
            for r in range(PHASES):
                src = pl.ds(r + PHASES * stage * c, stage, stride=PHASES)
                dst = pl.ds(pl.multiple_of(stage * c, stage), stage)
                k4_scr[r, dst, :] = k_ref[src, :]
                v = v_ref[src, :]
                va_scr[r, dst, :] = jnp.where(head0_s, v, 1.0)
                vb_scr[r, dst, :] = jnp.where(head0_s, 1.0, v)
            return carry
        lax.fori_loop(0, n4 // stage, body, 0)

    for r in range(PHASES):
        for c in range(t4 // stage):
            q = q_ref[pl.ds(r + PHASES * stage * c, stage, stride=PHASES), :] * (LOG2_E * HEAD_DIM ** -0.5)
            qa_scr[r, c * stage:(c + 1) * stage, :] = jnp.where(head0_s, q, 0.0)
            qb_scr[r, c * stage:(c + 1) * stage, :] = jnp.where(head0_s, 0.0, q)

    def pieces(scr, start, n):
        return jnp.concatenate([scr[r, pl.ds(start, n), :] for r in range(PHASES)], axis=0)

    def sub_block(b, idx):
        d = DIL_RATES[b]
        n_m = seq_len // d
        if d == 1:
            m0 = t * ATT_TILE + Q_BLK * idx
        elif d == PHASES:
            r, blk = idx % PHASES, idx // PHASES
            m0 = t * t4 + Q_BLK * blk
        else:
            r, r16 = idx % PHASES, idx // PHASES
            m0 = t * (ATT_TILE // d)
        k_start = jnp.clip(m0 - RADIUS, 0, n_m - K_BLK)
        variant = jnp.where(m0 < RADIUS, 0, jnp.where(m0 > n_m - Q_BLK - RADIUS, 2, 1))
        if d == 1:
            q0 = (Q_BLK // PHASES) * idx
            k0 = pl.multiple_of(k_start // PHASES, SUBLANES)
            load_q = lambda scr: pieces(scr, q0, Q_BLK // PHASES)
            load_k = lambda scr: pieces(scr, k0, K_BLK // PHASES)

            def store(scr, val):
                n = Q_BLK // PHASES
                for rr in range(PHASES):
                    scr[b, rr, q0:q0 + n, :] = val[rr * n:(rr + 1) * n]
        elif d == PHASES:
            load_q = lambda scr: scr[r, blk * Q_BLK:(blk + 1) * Q_BLK, :]
            load_k = lambda scr: scr[r, pl.ds(pl.multiple_of(k_start, SUBLANES), K_BLK), :]

            def store(scr, val):
                scr[b, r, blk * Q_BLK:(blk + 1) * Q_BLK, :] = val
        else:
            e = d // PHASES
            load_q = lambda scr: scr.at[r][pl.ds(r16, Q_BLK, stride=e), :]
            load_k = lambda scr: scr.at[r][pl.ds(r16 + e * k_start, K_BLK, stride=e), :]

            def store(scr, val):
                scr.at[b, r][pl.ds(r16, Q_BLK, stride=e), :] = val
        return load_q, load_k, store, variant

    for b in range(len(DIL_RATES)):
        for g0 in range(0, n_sub, ATT_GROUP):
            stores, key_loaders = [], []
            for j in range(ATT_GROUP):
                load_q, load_k, store, variant = sub_block(b, g0 + j)
                stores.append(store)
                key_loaders.append(load_k)
                kb = load_k(k4_scr).astype(BF16)
                for h2, q_scr in enumerate((qa_scr, qb_scr)):
                    s = lax.dot_general(load_q(q_scr).astype(BF16), kb, (((1,), (1,)), ((), ())),
                                        preferred_element_type=F32)
                    s_scr[j * HEADS_PER_STEP + h2] = (
                        s + tab_ref[0, (b * N_VARIANTS + variant) * HEADS_PER_STEP + h2])
            for j in range(ATT_GROUP):
                ms = []
                for h2 in range(HEADS_PER_STEP):
                    s = s_scr[j * HEADS_PER_STEP + h2]
                    m = jnp.max(s, axis=-1, keepdims=True)
                    p_scr[j * HEADS_PER_STEP + h2] = jnp.exp2((s - m).astype(BF16))
                    ms.append(m)
                stores[j](m_scr, jnp.where(head0, ms[0], ms[1]))
            for j in range(ATT_GROUP):
                outs = [jnp.dot(p_scr[j * HEADS_PER_STEP + h2], key_loaders[j](vh_scr).astype(BF16),
                                preferred_element_type=F32)
                        for h2, vh_scr in enumerate((va_scr, vb_scr))]
                stores[j](acc_scr, jnp.where(head0, outs[0], outs[1]))
                stores[j](den_scr, jnp.where(head0, outs[1], outs[0]))

    for r in range(PHASES):
        m_all = m_scr[:, r]
        m = jnp.max(m_all, axis=0)
        num = jnp.zeros((t4, LANES), F32)
        den = jnp.zeros((t4, LANES), F32)
        for b in range(len(DIL_RATES)):
            w = jnp.exp2(m_all[b] - m)
            num = num + w * acc_scr[b, r]
            den = den + w * pltpu.roll(den_scr[b, r], HEAD_DIM, axis=1)
        o_ref[pl.ds(r, t4, stride=PHASES), :] = num / den


def _attention(qkv, tab, n_seq, seq_len):
    n_hp = ATT_HEADS // HEADS_PER_STEP
    n_t = seq_len // ATT_TILE
    n_br = len(DIL_RATES)
    return pl.pallas_call(
        functools.partial(_att_kernel, seq_len=seq_len),
        grid=(n_seq, n_hp, n_t),
        in_specs=[
            pl.BlockSpec((ATT_TILE, LANES), lambda s, h, t: (s * n_t + t, h)),
            pl.BlockSpec((seq_len, LANES), lambda s, h, t: (s, n_hp + h)),
            pl.BlockSpec((seq_len, LANES), lambda s, h, t: (s, 2 * n_hp + h)),
            pl.BlockSpec((1,) + tab.shape[1:], lambda s, h, t: (h, 0, 0, 0)),
        ],
        out_specs=pl.BlockSpec((ATT_TILE, LANES), lambda s, h, t: (s * n_t + t, h)),
        out_shape=jax.ShapeDtypeStruct((n_seq * seq_len, ATT_WIDTH), F32),
        scratch_shapes=[
            pltpu.VMEM((PHASES, seq_len // PHASES, LANES), F32),
            pltpu.VMEM((PHASES, seq_len // PHASES, LANES), F32),
            pltpu.VMEM((PHASES, seq_len // PHASES, LANES), F32),
            pltpu.VMEM((PHASES, ATT_TILE // PHASES, LANES), F32),
            pltpu.VMEM((PHASES, ATT_TILE // PHASES, LANES), F32),
            pltpu.VMEM((n_br, PHASES, ATT_TILE // PHASES, LANES), F32),
            pltpu.VMEM((n_br, PHASES, ATT_TILE // PHASES, LANES), F32),
            pltpu.VMEM((n_br, PHASES, ATT_TILE // PHASES, LANES), F32),
            pltpu.VMEM((ATT_GROUP * HEADS_PER_STEP, Q_BLK, K_BLK), F32),
            pltpu.VMEM((ATT_GROUP * HEADS_PER_STEP, Q_BLK, K_BLK), BF16),
        ],
        compiler_params=pltpu.CompilerParams(
            dimension_semantics=("arbitrary", "arbitrary", "arbitrary"),
            vmem_limit_bytes=VMEM_LIMIT),
        name="dilated_attention",
    )(qkv, qkv, qkv, tab)


def _ssm_weights(a_re, a_im, log_dt, b_re, b_im, c_re, c_im, d_skip):
    hi = lax.Precision.HIGHEST
    G, N, HC = SSM_GROUPS, SSM_STATE, SSM_GROUP_CH
    dt = jnp.exp(log_dt)[..., None]
    mag = jnp.exp(a_re * dt)
    ab_re = mag * jnp.cos(a_im * dt)
    ab_im = mag * jnp.sin(a_im * dt)
    inv = 1.0 / (a_re * a_re + a_im * a_im)
    f_re = ((ab_re - 1.0) * a_re + ab_im * a_im) * inv
    f_im = (ab_im * a_re - (ab_re - 1.0) * a_im) * inv
    bb_re = f_re[..., None] * b_re - f_im[..., None] * b_im
    bb_im = f_re[..., None] * b_im + f_im[..., None] * b_re
    tau = jnp.arange(CHUNK + 1, dtype=F32)
    mag_t = jnp.exp((a_re * dt)[..., None] * tau)
    pr = mag_t * jnp.cos((a_im * dt)[..., None] * tau)
    pi = mag_t * jnp.sin((a_im * dt)[..., None] * tau)

    def a_pow_b(direction, rev):
        order = slice(CHUNK - 1, None, -1) if rev else slice(0, CHUNK)
        pr_c = jnp.repeat(pr[direction][..., order], HC, axis=-1)
        pi_c = jnp.repeat(pi[direction][..., order], HC, axis=-1)
        br = jnp.tile(bb_re[direction], (1, 1, CHUNK))
        bi = jnp.tile(bb_im[direction], (1, 1, CHUNK))
        return pr_c * br - pi_c * bi, pr_c * bi + pi_c * br

    def lag_table(direction, ab):
        return (jnp.einsum('gcn,gnx->gcx', c_re[direction], ab[0], precision=hi)
                - jnp.einsum('gcn,gnx->gcx', c_im[direction], ab[1], precision=hi))

    ab_f = a_pow_b(0, True)
    ab_b = a_pow_b(1, False)
    lag = jnp.stack([lag_table(0, ab_f), lag_table(1, ab_b)], axis=1)
    w_state = jnp.concatenate([ab_f[0], ab_f[1], ab_b[0], ab_b[1]], axis=1).astype(BF16)
    pr = jnp.moveaxis(pr, -1, 0)
    pi = jnp.moveaxis(pi, -1, 0)

    def state_out(direction, pr_sel, pi_sel):
        cr = c_re[direction][:, None, :, :]
        ci = c_im[direction][:, None, :, :]
        pr_s = pr_sel.transpose(1, 0, 2)[:, :, None, :]
        pi_s = pi_sel.transpose(1, 0, 2)[:, :, None, :]
        from_re = (cr * pr_s - ci * pi_s).reshape(G, CHUNK_COLS, N)
        from_im = (-cr * pi_s - ci * pr_s).reshape(G, CHUNK_COLS, N)
        return jnp.concatenate([from_re, from_im], axis=2)

    c_pow = jnp.concatenate(
        [state_out(0, pr[1:CHUNK + 1, 0], pi[1:CHUNK + 1, 0]),
         state_out(1, pr[CHUNK:0:-1, 1], pi[CHUNK:0:-1, 1])], axis=2).astype(BF16)

    a_chunk = jnp.stack([pr[CHUNK, 0], pi[CHUNK, 0], pr[CHUNK, 1], pi[CHUNK, 1]], axis=1)
    a_chunk = a_chunk.reshape(G // SSM_PAIR, SSM_PAIR, 4, N).transpose(0, 2, 1, 3).reshape(
        G // SSM_PAIR, 4, SSM_PAIR * N)
    d_col = jnp.tile(d_skip[:, None, :], (1, CHUNK, 1)).reshape(G, CHUNK_COLS, 1)
    return lag, w_state, c_pow, a_chunk, d_col


def _ssm_kernel(utp_ref, uts_ref, lag_ref, ws_ref, cp_ref, a_ref, d_ref, ytp_ref, yts_ref, g_scr, h_scr,
                *, n_chunks, n_p, n_s):
    seqs = [(utp_ref, ytp_ref, s) for s in range(n_p)] + [(uts_ref, yts_ref, s) for s in range(n_s)]
    n_st = SSM_STATE
    pair = range(SSM_PAIR)

    def group_rows(q):
        return slice(q * SSM_GROUP_CH, (q + 1) * SSM_GROUP_CH)

    @pl.when(pl.program_id(0) == 0)
    def _():
        g_scr[...] = jnp.zeros(g_scr.shape, F32)

    col_j = lax.broadcasted_iota(jnp.int32, (SSM_GROUP_CH, CHUNK_COLS), 1) // SSM_GROUP_CH
    w_intra = []
    for q in pair:
        lag_f, lag_b = lag_ref[q, 0], lag_ref[q, 1]
        blocks = []
        for i in range(CHUNK):
            fwd = pltpu.roll(lag_f, (i + 1) * SSM_GROUP_CH % CHUNK_COLS, axis=1)
            bwd = pltpu.roll(lag_b, i * SSM_GROUP_CH, axis=1)
            blocks.append(jnp.where(col_j <= i, fwd, 0.0) + jnp.where(col_j >= i, bwd, 0.0))
        w_intra.append(jnp.concatenate(blocks, axis=0).astype(BF16))

    for slot, (u_ref, y_ref, s) in enumerate(seqs):
        lanes = slice(s * n_chunks, (s + 1) * n_chunks)
        state_in = []
        for q in pair:
            x_t = u_ref[:, group_rows(q), lanes].reshape(CHUNK_COLS, n_chunks).astype(BF16)
            r = jnp.dot(w_intra[q], x_t, preferred_element_type=F32)
            y_ref[:, group_rows(q), lanes] = r.reshape(CHUNK, SSM_GROUP_CH, n_chunks)
            state_in.append(jnp.dot(ws_ref[q], x_t, preferred_element_type=F32))
        for k in range(4):
            both = jnp.concatenate([state_in[q][k * n_st:(k + 1) * n_st] for q in pair], axis=0)
            g_scr.at[k][pl.ds(slot, n_chunks, stride=SEQ_PAD), :] = both.T

    a = a_ref[0]
    shape = (SEQ_PAD, LANES)
    prf, pif, prb, pib = [jnp.broadcast_to(a[k:k + 1], shape) for k in range(4)]

    def step(c, carry):
        fr, fi, br, bi = carry
        rf = pl.ds(pl.multiple_of(c * SEQ_PAD, SEQ_PAD), SEQ_PAD)
        rb = pl.ds(pl.multiple_of((n_chunks - 1 - c) * SEQ_PAD, SEQ_PAD), SEQ_PAD)
        h_scr[0, rf, :] = fr
        h_scr[1, rf, :] = fi
        h_scr[2, rb, :] = br
        h_scr[3, rb, :] = bi
        return (prf * fr - pif * fi + g_scr[0, rf, :], prf * fi + pif * fr + g_scr[1, rf, :],
                prb * br - pib * bi + g_scr[2, rb, :], prb * bi + pib * br + g_scr[3, rb, :])

    zero = jnp.zeros(shape, F32)
    lax.fori_loop(0, n_chunks, step, (zero, zero, zero, zero), unroll=SCAN_UNROLL)

    for slot, (u_ref, y_ref, s) in enumerate(seqs):
        lanes = slice(s * n_chunks, (s + 1) * n_chunks)
        h_ts = [h_scr.at[k][pl.ds(slot, n_chunks, stride=SEQ_PAD), :].T for k in range(4)]
        for q in pair:
            h_t = jnp.concatenate([h[q * n_st:(q + 1) * n_st] for h in h_ts], axis=0)
            x_t = u_ref[:, group_rows(q), lanes].reshape(CHUNK_COLS, n_chunks)
            y = (y_ref[:, group_rows(q), lanes].reshape(CHUNK_COLS, n_chunks)
                 + jnp.dot(cp_ref[q], h_t.astype(BF16), preferred_element_type=F32) + d_ref[q] * x_t)
            y_ref[:, group_rows(q), lanes] = y.reshape(CHUNK, SSM_GROUP_CH, n_chunks)


def _ssm(ut_p, ut_s, lag, w_state, c_pow, a_chunk, d_col, n_chunks):
    G = SSM_GROUPS // SSM_PAIR
    n_p = ut_p.shape[2] // n_chunks
    n_s = ut_s.shape[2] // n_chunks
    n_rows = n_chunks * SEQ_PAD
    act = lambda arr: pl.BlockSpec((CHUNK, SSM_PAIR * SSM_GROUP_CH, arr.shape[2]), lambda g: (0, g, 0))
    per_group = lambda arr: pl.BlockSpec(
        (arr.shape[0] // G,) + arr.shape[1:], lambda g: (g,) + (0,) * (arr.ndim - 1))
    return pl.pallas_call(
        functools.partial(_ssm_kernel, n_chunks=n_chunks, n_p=n_p, n_s=n_s),
        grid=(G,),
        in_specs=[act(ut_p), act(ut_s), per_group(lag), per_group(w_state), per_group(c_pow),
                  per_group(a_chunk), per_group(d_col)],
        out_specs=[act(ut_p), act(ut_s)],
        out_shape=[jax.ShapeDtypeStruct(ut_p.shape, F32), jax.ShapeDtypeStruct(ut_s.shape, F32)],
        scratch_shapes=[
            pltpu.VMEM((4, n_rows, LANES), F32),
            pltpu.VMEM((4, n_rows, LANES), F32),
        ],
        compiler_params=pltpu.CompilerParams(
            dimension_semantics=("arbitrary",), vmem_limit_bytes=VMEM_LIMIT),
        name="ssm_chunked",
    )(ut_p, ut_s, lag, w_state, c_pow, a_chunk, d_col)


def _post_kernel(x_ref, att_ref, yt_ref, p_ref, wglu_ref, bglu_ref, gatt_ref, gssm_ref, wout_ref,
                 gmlp_ref, w1_ref, w2_ref, gple_ref, wgate_ref, wproj_ref, gfin_ref, o_ref, ys_scr):
    k = pl.program_id(1)

    @pl.when(k == 0)
    def _():
        for l in range(SSM_WIDTH // LANES):
            for i in range(CHUNK):
                ys_scr.at[l][pl.ds(i, LANES, stride=CHUNK), :] = yt_ref[i, l * LANES:(l + 1) * LANES, :].T

    rows = pl.ds(pl.multiple_of(k * POST_TILE, POST_TILE), POST_TILE)
    ys = jnp.concatenate([ys_scr[l, rows, :] for l in range(SSM_WIDTH // LANES)], axis=1)
    g = _gelu_tanh(ys)
    ssm = g * _sigmoid(_bdot(g, wglu_ref[...]) + bglu_ref[...])
    att_n = _rms(att_ref[...], gatt_ref[...])
    ssm_n = _rms(ssm, gssm_ref[...])
    h = x_ref[...] + (_bdot(att_n, wout_ref[:ATT_WIDTH, :]) + _bdot(ssm_n, wout_ref[ATT_WIDTH:, :]))
    f = _rms(h, gmlp_ref[...]).astype(BF16)
    acc = jnp.zeros_like(h)
    for kb in range(D_FF // FF_BLK):
        cols = slice(kb * FF_BLK, (kb + 1) * FF_BLK)
        t = jnp.dot(f, w1_ref[:, cols], preferred_element_type=F32)
        t = jnp.square(jnp.maximum(t, 0.0))
        acc = acc + _bdot(t, w2_ref[cols, :])
    h = h + acc
    e = _rms(h, gple_ref[...])
    h = h + _sigmoid(_bdot(e, wgate_ref[...])) * _bdot(p_ref[...], wproj_ref[...])
    o_ref[...] = _rms(h, gfin_ref[...])


def _post(x, att, yt, p, wts):
    t = x.shape[0]
    n_in = SUPER_TILE // POST_TILE
    tile = lambda width: pl.BlockSpec((POST_TILE, width), lambda i, k: (i * n_in + k, 0))

    def resident(arr):
        return pl.BlockSpec(arr.shape, lambda i, k: (0,) * arr.ndim, pipeline_mode=pl.Buffered(1))

    return pl.pallas_call(
        _post_kernel,
        grid=(t // SUPER_TILE, n_in),
        in_specs=[tile(D_MODEL), tile(ATT_WIDTH),
                  pl.BlockSpec((CHUNK, SSM_WIDTH, LANES), lambda i, k: (0, 0, i)), tile(PLE_DIM)]
                 + [resident(w) for w in wts],
        out_specs=tile(D_MODEL),
        out_shape=jax.ShapeDtypeStruct((t, D_MODEL), F32),
        scratch_shapes=[pltpu.VMEM((SSM_WIDTH // LANES, SUPER_TILE, LANES), F32)],
        compiler_params=pltpu.CompilerParams(
            dimension_semantics=("arbitrary", "arbitrary"), vmem_limit_bytes=VMEM_LIMIT),
        name="post_mixers",
    )(x, att, yt, p, *wts)


def kernel(x_prompt, x_sample, p_prompt, p_sample, rel_bias, g_mix, w_in, ssm_a_re, ssm_a_im, ssm_log_dt, ssm_b_re, ssm_b_im, ssm_c_re, ssm_c_im, ssm_d, w_glu, b_glu, g_att_out, g_ssm_out, w_out, g_mlp, w_mlp1, w_mlp2, g_ple, w_ple_gate, w_ple_proj, g_final):
    assert g_mix.shape[0] == 1, "single-layer trunk"
    seq_len = x_prompt.shape[1]
    assert x_sample.shape[1] == seq_len and seq_len % ATT_TILE == 0
    n_p, n_s = x_prompt.shape[0], x_sample.shape[0]
    assert n_p + n_s <= SEQ_PAD
    n_chunks = seq_len // CHUNK

    w_in_bf = w_in[0].astype(BF16)
    tab = _bias_tables(rel_bias)
    ssm_wts = _ssm_weights(
        ssm_a_re[0], ssm_a_im[0], ssm_log_dt[0], ssm_b_re[0], ssm_b_im[0],
        ssm_c_re[0], ssm_c_im[0], ssm_d[0])
    row = lambda v, n: v.reshape(1, n)
    wts = (w_glu[0].astype(BF16), row(b_glu[0], SSM_WIDTH), row(g_att_out[0], ATT_WIDTH),
           row(g_ssm_out[0], SSM_WIDTH), w_out[0].astype(BF16), row(g_mlp[0], D_MODEL),
           w_mlp1[0].astype(BF16), w_mlp2[0].astype(BF16), row(g_ple[0], D_MODEL),
           w_ple_gate[0].astype(BF16), w_ple_proj[0].astype(BF16), row(g_final, D_MODEL))

    groups = []
    for x3, p4 in ((x_prompt, p_prompt), (x_sample, p_sample)):
        n = x3.shape[0]
        x = x3.reshape(n * seq_len, D_MODEL)
        qkv, ut = _inproj(x, g_mix[0], w_in_bf)
        att = _attention(qkv, tab, n, seq_len)
        groups.append((x, p4[0].reshape(n * seq_len, PLE_DIM), att, ut))
    yts = _ssm(groups[0][3], groups[1][3], *ssm_wts, n_chunks)
    outs = [_post(x, att, yt, p, wts).reshape(-1, seq_len, D_MODEL)
            for (x, p, att, _), yt in zip(groups, yts)]
    return outs[0], outs[1]
```

```python
import functools
import math

import jax
import jax.numpy as jnp
import numpy as np
from jax import lax
from jax.experimental import pallas as pl
from jax.experimental.pallas import tpu as pltpu

F32 = jnp.float32
BF16 = jnp.bfloat16

D_MODEL = 1024
ATT_HEADS = 8
HEAD_DIM = 64
ATT_WIDTH = ATT_HEADS * HEAD_DIM
SSM_WIDTH = D_MODEL - ATT_WIDTH
SSM_GROUP_CH = 16
SSM_GROUPS = SSM_WIDTH // SSM_GROUP_CH
SSM_STATE = 64
IN_COLS = 3 * ATT_WIDTH + SSM_WIDTH
D_FF = 4 * D_MODEL
PLE_DIM = 256
NUM_BUCKETS = 32
REL_MAX_DISTANCE = 1024
DIL_WINDOWS = (128, 512, 2048)
DIL_RATES = (1, 4, 16)
RMS_EPS = 1e-6
NEG_INF = -1e30
LOG2_E = math.log2(math.e)

LANES = 128
SUBLANES = 8
VMEM_LIMIT = 56 * 1024 * 1024

RADIUS = 64
Q_BLK = 128
K_BLK = Q_BLK + 2 * RADIUS
ATT_TILE = 2048
PHASES = 4
HEADS_PER_STEP = LANES // HEAD_DIM
ATT_GROUP = 16
N_VARIANTS = 3

CHUNK = 16
CHUNK_COLS = CHUNK * SSM_GROUP_CH
SEQ_PAD = SUBLANES
SSM_PAIR = LANES // SSM_STATE
SCAN_UNROLL = 16

SUPER_TILE = CHUNK * LANES
IN_TILE = 1024
POST_TILE = 512
FF_BLK = 1024


def _rms(x, g):
    return x * lax.rsqrt(jnp.mean(x * x, axis=-1, keepdims=True) + RMS_EPS) * g


def _sigmoid(x):
    return 1.0 / (1.0 + jnp.exp(-x))


def _gelu_tanh(x):
    c = math.sqrt(2.0 / math.pi)
    return 0.5 * x * (1.0 + jnp.tanh(c * (x + 0.044715 * (x * x * x))))


def _bdot(a, b):
    return jnp.dot(a.astype(BF16), b.astype(BF16), preferred_element_type=F32)


def _inproj_kernel(x_ref, g_ref, w_ref, qkv_ref, ut_ref, u_scr):
    k = pl.program_id(1)
    a = _rms(x_ref[...], g_ref[...])
    z = _bdot(a, w_ref[...])
    qkv_ref[...] = z[:, :3 * ATT_WIDTH]
    rows = pl.ds(pl.multiple_of(k * IN_TILE, IN_TILE), IN_TILE)
    for l in range(SSM_WIDTH // LANES):
        u_scr[l, rows, :] = z[:, 3 * ATT_WIDTH + l * LANES:3 * ATT_WIDTH + (l + 1) * LANES]

    @pl.when(k == SUPER_TILE // IN_TILE - 1)
    def _():
        for l in range(SSM_WIDTH // LANES):
            for j in range(CHUNK):
                ut_ref[j, l * LANES:(l + 1) * LANES, :] = u_scr.at[l][pl.ds(j, LANES, stride=CHUNK), :].T


def _inproj(x, g_mix, w_in_bf):
    t = x.shape[0]
    n_in = SUPER_TILE // IN_TILE
    return pl.pallas_call(
        _inproj_kernel,
        grid=(t // SUPER_TILE, n_in),
        in_specs=[
            pl.BlockSpec((IN_TILE, D_MODEL), lambda i, k: (i * n_in + k, 0)),
            pl.BlockSpec((1, D_MODEL), lambda i, k: (0, 0)),
            pl.BlockSpec((D_MODEL, IN_COLS), lambda i, k: (0, 0)),
        ],
        out_specs=[
            pl.BlockSpec((IN_TILE, 3 * ATT_WIDTH), lambda i, k: (i * n_in + k, 0)),
            pl.BlockSpec((CHUNK, SSM_WIDTH, LANES), lambda i, k: (0, 0, i)),
        ],
        out_shape=[
            jax.ShapeDtypeStruct((t, 3 * ATT_WIDTH), F32),
            jax.ShapeDtypeStruct((CHUNK, SSM_WIDTH, t // CHUNK), F32),
        ],
        scratch_shapes=[pltpu.VMEM((SSM_WIDTH // LANES, SUPER_TILE, LANES), F32)],
        compiler_params=pltpu.CompilerParams(
            dimension_semantics=("arbitrary", "arbitrary"), vmem_limit_bytes=VMEM_LIMIT),
        name="inproj",
    )(x, g_mix.reshape(1, D_MODEL), w_in_bf)


def _t5_bucket_np(rel):
    half = NUM_BUCKETS // 2
    n = -rel
    ret = np.where(n < 0, half, 0)
    n = np.abs(n)
    max_exact = half // 2
    nf = np.maximum(n, 1).astype(np.float64)
    large = max_exact + (np.log(nf / max_exact) / math.log(REL_MAX_DISTANCE / max_exact)
                         * (half - max_exact)).astype(np.int64)
    large = np.minimum(large, half - 1)
    return ret + np.where(n < max_exact, n, large)


def _bucket_tables():
    out = np.zeros((len(DIL_RATES), N_VARIANTS, Q_BLK, K_BLK), np.int32)
    for b, d in enumerate(DIL_RATES):
        qi = np.arange(Q_BLK)
        ci = np.arange(K_BLK)
        if d == 1:
            qi = (qi % (Q_BLK // PHASES)) * PHASES + qi // (Q_BLK // PHASES)
            ci = (ci % (K_BLK // PHASES)) * PHASES + ci // (K_BLK // PHASES)
        for v, shift in enumerate((0, -RADIUS, -2 * RADIUS)):
            off = ci[None, :] - qi[:, None] + shift
            bk = _t5_bucket_np(off * d)
            out[b, v] = np.where(np.abs(off) <= RADIUS, bk, NUM_BUCKETS)
    return out.reshape(len(DIL_RATES) * N_VARIANTS, Q_BLK, K_BLK)


def _bias_kernel(rel_ref, bk_ref, tab_ref, *, buckets):
    hp = pl.program_id(0)
    for bv in range(buckets.shape[0]):
        for r0 in range(0, Q_BLK, SUBLANES):
            for c0 in range(0, K_BLK, LANES):
                rows, cols = slice(r0, r0 + SUBLANES), slice(c0, c0 + LANES)
                bk = bk_ref[bv, rows, cols]
                present = [int(b) for b in np.unique(buckets[bv, rows, cols]) if b < NUM_BUCKETS]
                hits = [bk == b for b in present]
                for h2 in range(HEADS_PER_STEP):
                    acc = jnp.full(bk.shape, NEG_INF, F32)
                    for b, hit in zip(present, hits):
                        acc = jnp.where(hit, rel_ref[b, hp * HEADS_PER_STEP + h2] * LOG2_E, acc)
                    tab_ref[0, bv * HEADS_PER_STEP + h2, rows, cols] = acc


def _bias_tables(rel_bias):
    buckets = _bucket_tables()
    bk = jnp.asarray(buckets)
    n_bv = bk.shape[0]
    n_hp = ATT_HEADS // HEADS_PER_STEP
    return pl.pallas_call(
        functools.partial(_bias_kernel, buckets=buckets),
        grid=(n_hp,),
        in_specs=[
            pl.BlockSpec(memory_space=pltpu.SMEM),
            pl.BlockSpec((n_bv, Q_BLK, K_BLK), lambda h: (0, 0, 0)),
        ],
        out_specs=pl.BlockSpec((1, n_bv * HEADS_PER_STEP, Q_BLK, K_BLK), lambda h: (h, 0, 0, 0)),
        out_shape=jax.ShapeDtypeStruct((n_hp, n_bv * HEADS_PER_STEP, Q_BLK, K_BLK), F32),
        compiler_params=pltpu.CompilerParams(dimension_semantics=("arbitrary",)),
        name="bias_tables",
    )(rel_bias, bk)


def _att_kernel(q_ref, k_ref, v_ref, tab_ref, o_ref, k4_scr, va_scr, vb_scr, qa_scr, qb_scr,
                acc_scr, m_scr, den_scr, s_scr, p_scr, *, seq_len):
    t = pl.program_id(2)
    n4 = seq_len // PHASES
    t4 = ATT_TILE // PHASES
    n_sub = ATT_TILE // Q_BLK
    stage = 256

    def head0_mask(rows):
        return lax.broadcasted_iota(jnp.int32, (rows, LANES), 1) < HEAD_DIM

    head0 = head0_mask(Q_BLK)
    head0_s = head0_mask(stage)

    @pl.when(t == 0)
    def _():
        def body(c, carry):
            for r in range(PHASES):
                src = pl.ds(r + PHASES * stage * c, stage, stride=PHASES)
                dst = pl.ds(pl.multiple_of(stage * c, stage), stage)
                k4_scr[r, dst, :] = k_ref[src, :]
                v = v_ref[src, :]
                va_scr[r, dst, :] = jnp.where(head0_s, v, 1.0)
                vb_scr[r, dst, :] = jnp.where(head0_s, 1.0, v)
            return carry
        lax.fori_loop(0, n4 // stage, body, 0)

    for r in range(PHASES):
        for c in range(t4 // stage):
            q = q_ref[pl.ds(r + PHASES * stage * c, stage, stride=PHASES), :] * (LOG2_E * HEAD_DIM ** -0.5)
            qa_scr[r, c * stage:(c + 1) * stage, :] = jnp.where(head0_s, q, 0.0)
            qb_scr[r, c * stage:(c + 1) * stage, :] = jnp.where(head0_s, 0.0, q)

    def pieces(scr, start, n):
        return jnp.concatenate([scr[r, pl.ds(start, n), :] for r in range(PHASES)], axis=0)

    def sub_block(b, idx):
        d = DIL_RATES[b]
        n_m = seq_len // d
        if d == 1:
            m0 = t * ATT_TILE + Q_BLK * idx
        elif d == PHASES:
            r, blk = idx % PHASES, idx // PHASES
            m0 = t * t4 + Q_BLK * blk
        else:
            r, r16 = idx % PHASES, idx // PHASES
            m0 = t * (ATT_TILE // d)
        k_start = jnp.clip(m0 - RADIUS, 0, n_m - K_BLK)
        variant = jnp.where(m0 < RADIUS, 0, jnp.where(m0 > n_m - Q_BLK - RADIUS, 2, 1))
        if d == 1:
            q0 = (Q_BLK // PHASES) * idx
            k0 = pl.multiple_of(k_start // PHASES, SUBLANES)
            load_q = lambda scr: pieces(scr, q0, Q_BLK // PHASES)
            load_k = lambda scr: pieces(scr, k0, K_BLK // PHASES)

            def store(scr, val):
                n = Q_BLK // PHASES
                for rr in range(PHASES):
                    scr[b, rr, q0:q0 + n, :] = val[rr * n:(rr + 1) * n]
        elif d == PHASES:
            load_q = lambda scr: scr[r, blk * Q_BLK:(blk + 1) * Q_BLK, :]
            load_k = lambda scr: scr[r, pl.ds(pl.multiple_of(k_start, SUBLANES), K_BLK), :]

            def store(scr, val):
                scr[b, r, blk * Q_BLK:(blk + 1) * Q_BLK, :] = val
        else:
            e = d // PHASES
            load_q = lambda scr: scr.at[r][pl.ds(r16, Q_BLK, stride=e), :]
            load_k = lambda scr: scr.at[r][pl.ds(r16 + e * k_start, K_BLK, stride=e), :]

            def store(scr, val):
                scr.at[b, r][pl.ds(r16, Q_BLK, stride=e), :] = val
        return load_q, load_k, store, variant

    for b in range(len(DIL_RATES)):
        for g0 in range(0, n_sub, ATT_GROUP):
            stores, key_loaders = [], []
            for j in range(ATT_GROUP):
                load_q, load_k, store, variant = sub_block(b, g0 + j)
                stores.append(store)
                key_loaders.append(load_k)
                kb = load_k(k4_scr).astype(BF16)
                for h2, q_scr in enumerate((qa_scr, qb_scr)):
                    s = lax.dot_general(load_q(q_scr).astype(BF16), kb, (((1,), (1,)), ((), ())),
                                        preferred_element_type=F32)
                    s_scr[j * HEADS_PER_STEP + h2] = (
                        s + tab_ref[0, (b * N_VARIANTS + variant) * HEADS_PER_STEP + h2])
            for j in range(ATT_GROUP):
                ms = []
                for h2 in range(HEADS_PER_STEP):
                    s = s_scr[j * HEADS_PER_STEP + h2]
                    m = jnp.max(s, axis=-1, keepdims=True)
                    p_scr[j * HEADS_PER_STEP + h2] = jnp.exp2((s - m).astype(BF16))
                    ms.append(m)
                stores[j](m_scr, jnp.where(head0, ms[0], ms[1]))
            for j in range(ATT_GROUP):
                outs = [jnp.dot(p_scr[j * HEADS_PER_STEP + h2], key_loaders[j](vh_scr).astype(BF16),
                                preferred_element_type=F32)
                        for h2, vh_scr in enumerate((va_scr, vb_scr))]
                stores[j](acc_scr, jnp.where(head0, outs[0], outs[1]))
                stores[j](den_scr, jnp.where(head0, outs[1], outs[0]))

    for r in range(PHASES):
        m_all = m_scr[:, r]
        m = jnp.max(m_all, axis=0)
        num = jnp.zeros((t4, LANES), F32)
        den = jnp.zeros((t4, LANES), F32)
        for b in range(len(DIL_RATES)):
            w = jnp.exp2(m_all[b] - m)
            num = num + w * acc_scr[b, r]
            den = den + w * pltpu.roll(den_scr[b, r], HEAD_DIM, axis=1)
        o_ref[pl.ds(r, t4, stride=PHASES), :] = num / den


def _attention(qkv, tab, n_seq, seq_len):
    n_hp = ATT_HEADS // HEADS_PER_STEP
    n_t = seq_len // ATT_TILE
    n_br = len(DIL_RATES)
    return pl.pallas_call(
        functools.partial(_att_kernel, seq_len=seq_len),
        grid=(n_seq, n_hp, n_t),
        in_specs=[
            pl.BlockSpec((ATT_TILE, LANES), lambda s, h, t: (s * n_t + t, h)),
            pl.BlockSpec((seq_len, LANES), lambda s, h, t: (s, n_hp + h)),
            pl.BlockSpec((seq_len, LANES), lambda s, h, t: (s, 2 * n_hp + h)),
            pl.BlockSpec((1,) + tab.shape[1:], lambda s, h, t: (h, 0, 0, 0)),
        ],
        out_specs=pl.BlockSpec((ATT_TILE, LANES), lambda s, h, t: (s * n_t + t, h)),
        out_shape=jax.ShapeDtypeStruct((n_seq * seq_len, ATT_WIDTH), F32),
        scratch_shapes=[
            pltpu.VMEM((PHASES, seq_len // PHASES, LANES), F32),
            pltpu.VMEM((PHASES, seq_len // PHASES, LANES), F32),
            pltpu.VMEM((PHASES, seq_len // PHASES, LANES), F32),
            pltpu.VMEM((PHASES, ATT_TILE // PHASES, LANES), F32),
            pltpu.VMEM((PHASES, ATT_TILE // PHASES, LANES), F32),
            pltpu.VMEM((n_br, PHASES, ATT_TILE // PHASES, LANES), F32),
            pltpu.VMEM((n_br, PHASES, ATT_TILE // PHASES, LANES), F32),
            pltpu.VMEM((n_br, PHASES, ATT_TILE // PHASES, LANES), F32),
            pltpu.VMEM((ATT_GROUP * HEADS_PER_STEP, Q_BLK, K_BLK), F32),
            pltpu.VMEM((ATT_GROUP * HEADS_PER_STEP, Q_BLK, K_BLK), BF16),
        ],
        compiler_params=pltpu.CompilerParams(
            dimension_semantics=("arbitrary", "arbitrary", "arbitrary"),
            vmem_limit_bytes=VMEM_LIMIT),
        name="dilated_attention",
    )(qkv, qkv, qkv, tab)


def _ssm_weights(a_re, a_im, log_dt, b_re, b_im, c_re, c_im, d_skip):
    hi = lax.Precision.HIGHEST
    G, N, HC = SSM_GROUPS, SSM_STATE, SSM_GROUP_CH
    dt = jnp.exp(log_dt)[..., None]
    mag = jnp.exp(a_re * dt)
    ab_re = mag * jnp.cos(a_im * dt)
    ab_im = mag * jnp.sin(a_im * dt)
    inv = 1.0 / (a_re * a_re + a_im * a_im)
    f_re = ((ab_re - 1.0) * a_re + ab_im * a_im) * inv
    f_im = (ab_im * a_re - (ab_re - 1.0) * a_im) * inv
    bb_re = f_re[..., None] * b_re - f_im[..., None] * b_im
    bb_im = f_re[..., None] * b_im + f_im[..., None] * b_re
    lam_re, lam_im = a_re * dt, a_im * dt

    def a_pow(direction, taus, lead):
        t = jnp.asarray(np.asarray(taus, np.float32))
        t = t[None, :, None] if lead else t[None, None, :]
        lr = lam_re[direction][:, None, :] if lead else lam_re[direction][:, :, None]
        li = lam_im[direction][:, None, :] if lead else lam_im[direction][:, :, None]
        mag_t = jnp.exp(lr * t)
        return mag_t * jnp.cos(li * t), mag_t * jnp.sin(li * t)

    def a_pow_b(direction, rev):
        order = np.arange(CHUNK - 1, -1, -1) if rev else np.arange(CHUNK)
        pr_c, pi_c = a_pow(direction, np.repeat(order, HC), lead=False)
        br = jnp.tile(bb_re[direction], (1, 1, CHUNK))
        bi = jnp.tile(bb_im[direction], (1, 1, CHUNK))
        return pr_c * br - pi_c * bi, pr_c * bi + pi_c * br

    def lag_table(direction, ab):
        return (jnp.einsum('gcn,gnx->gcx', c_re[direction], ab[0], precision=hi)
                - jnp.einsum('gcn,gnx->gcx', c_im[direction], ab[1], precision=hi))

    ab_f = a_pow_b(0, True)
    ab_b = a_pow_b(1, False)
    lag = jnp.stack([lag_table(0, ab_f), lag_table(1, ab_b)], axis=1)
    w_state = jnp.concatenate([ab_f[0], ab_f[1], ab_b[0], ab_b[1]], axis=1).astype(BF16)

    def state_out(direction, taus):
        cr = c_re[direction][:, None, :, :]
        ci = c_im[direction][:, None, :, :]
        pr_s, pi_s = [p[:, :, None, :] for p in a_pow(direction, taus, lead=True)]
        from_re = (cr * pr_s - ci * pi_s).reshape(G, CHUNK_COLS, N)
        from_im = (-cr * pi_s - ci * pr_s).reshape(G, CHUNK_COLS, N)
        return jnp.concatenate([from_re, from_im], axis=2)

    c_pow = jnp.concatenate(
        [state_out(0, np.arange(1, CHUNK + 1)),
         state_out(1, np.arange(CHUNK, 0, -1))], axis=2).astype(BF16)

    a_f, a_b = [a_pow(direction, [CHUNK], lead=True) for direction in range(2)]
    a_chunk = jnp.concatenate([a_f[0], a_f[1], a_b[0], a_b[1]], axis=1)
    a_chunk = a_chunk.reshape(G // SSM_PAIR, SSM_PAIR, 4, N).transpose(0, 2, 1, 3).reshape(
        G // SSM_PAIR, 4, SSM_PAIR * N)
    d_col = jnp.tile(d_skip[:, None, :], (1, CHUNK, 1)).reshape(G, CHUNK_COLS, 1)
    return lag, w_state, c_pow, a_chunk, d_col


def _ssm_kernel(utp_ref, uts_ref, lag_ref, ws_ref, cp_ref, a_ref, d_ref, ytp_ref, yts_ref, g_scr, h_scr,
                *, n_chunks, n_p, n_s):
    seqs = [(utp_ref, ytp_ref, s) for s in range(n_p)] + [(uts_ref, yts_ref, s) for s in range(n_s)]
    n_st = SSM_STATE
    pair = range(SSM_PAIR)

    def group_rows(q):
        return slice(q * SSM_GROUP_CH, (q + 1) * SSM_GROUP_CH)

    @pl.when(pl.program_id(0) == 0)
    def _():
        g_scr[...] = jnp.zeros(g_scr.shape, F32)

    col_j = lax.broadcasted_iota(jnp.int32, (SSM_GROUP_CH, CHUNK_COLS), 1) // SSM_GROUP_CH
    w_intra = []
    for q in pair:
        lag_f, lag_b = lag_ref[q, 0], lag_ref[q, 1]
        blocks = []
        for i in range(CHUNK):
            fwd = pltpu.roll(lag_f, (i + 1) * SSM_GROUP_CH % CHUNK_COLS, axis=1)
            bwd = pltpu.roll(lag_b, i * SSM_GROUP_CH, axis=1)
            blocks.append(jnp.where(col_j <= i, fwd, 0.0) + jnp.where(col_j >= i, bwd, 0.0))
        w_intra.append(jnp.concatenate(blocks, axis=0).astype(BF16))

    for slot, (u_ref, y_ref, s) in enumerate(seqs):
        lanes = slice(s * n_chunks, (s + 1) * n_chunks)
        state_in = []
        for q in pair:
            x_t = u_ref[:, group_rows(q), lanes].reshape(CHUNK_COLS, n_chunks).astype(BF16)
            r = jnp.dot(w_intra[q], x_t, preferred_element_type=F32)
            y_ref[:, group_rows(q), lanes] = r.reshape(CHUNK, SSM_GROUP_CH, n_chunks)
            state_in.append(jnp.dot(ws_ref[q], x_t, preferred_element_type=F32))
        for k in range(4):
            both = jnp.concatenate([state_in[q][k * n_st:(k + 1) * n_st] for q in pair], axis=0)
            g_scr.at[k][pl.ds(slot, n_chunks, stride=SEQ_PAD), :] = both.T

    a = a_ref[0]
    shape = (SEQ_PAD, LANES)
    prf, pif, prb, pib = [jnp.broadcast_to(a[k:k + 1], shape) for k in range(4)]

    def step(c, carry):
        fr, fi, br, bi = carry
        rf = pl.ds(pl.multiple_of(c * SEQ_PAD, SEQ_PAD), SEQ_PAD)
        rb = pl.ds(pl.multiple_of((n_chunks - 1 - c) * SEQ_PAD, SEQ_PAD), SEQ_PAD)
        h_scr[0, rf, :] = fr
        h_scr[1, rf, :] = fi
        h_scr[2, rb, :] = br
        h_scr[3, rb, :] = bi
        return (prf * fr - pif * fi + g_scr[0, rf, :], prf * fi + pif * fr + g_scr[1, rf, :],
                prb * br - pib * bi + g_scr[2, rb, :], prb * bi + pib * br + g_scr[3, rb, :])

    zero = jnp.zeros(shape, F32)
    lax.fori_loop(0, n_chunks, step, (zero, zero, zero, zero), unroll=SCAN_UNROLL)

    for slot, (u_ref, y_ref, s) in enumerate(seqs):
        lanes = slice(s * n_chunks, (s + 1) * n_chunks)
        h_ts = [h_scr.at[k][pl.ds(slot, n_chunks, stride=SEQ_PAD), :].T for k in range(4)]
        for q in pair:
            h_t = jnp.concatenate([h[q * n_st:(q + 1) * n_st] for h in h_ts], axis=0)
            x_t = u_ref[:, group_rows(q), lanes].reshape(CHUNK_COLS, n_chunks)
            y = (y_ref[:, group_rows(q), lanes].reshape(CHUNK_COLS, n_chunks)
                 + jnp.dot(cp_ref[q], h_t.astype(BF16), preferred_element_type=F32) + d_ref[q] * x_t)
            y_ref[:, group_rows(q), lanes] = y.reshape(CHUNK, SSM_GROUP_CH, n_chunks)


def _ssm(ut_p, ut_s, lag, w_state, c_pow, a_chunk, d_col, n_chunks):
    G = SSM_GROUPS // SSM_PAIR
    n_p = ut_p.shape[2] // n_chunks
    n_s = ut_s.shape[2] // n_chunks
    n_rows = n_chunks * SEQ_PAD
    act = lambda arr: pl.BlockSpec((CHUNK, SSM_PAIR * SSM_GROUP_CH, arr.shape[2]), lambda g: (0, g, 0))
    per_group = lambda arr: pl.BlockSpec(
        (arr.shape[0] // G,) + arr.shape[1:], lambda g: (g,) + (0,) * (arr.ndim - 1))
    return pl.pallas_call(
        functools.partial(_ssm_kernel, n_chunks=n_chunks, n_p=n_p, n_s=n_s),
        grid=(G,),
        in_specs=[act(ut_p), act(ut_s), per_group(lag), per_group(w_state), per_group(c_pow),
                  per_group(a_chunk), per_group(d_col)],
        out_specs=[act(ut_p), act(ut_s)],
        out_shape=[jax.ShapeDtypeStruct(ut_p.shape, F32), jax.ShapeDtypeStruct(ut_s.shape, F32)],
        scratch_shapes=[
            pltpu.VMEM((4, n_rows, LANES), F32),
            pltpu.VMEM((4, n_rows, LANES), F32),
        ],
        compiler_params=pltpu.CompilerParams(
            dimension_semantics=("arbitrary",), vmem_limit_bytes=VMEM_LIMIT),
        name="ssm_chunked",
    )(ut_p, ut_s, lag, w_state, c_pow, a_chunk, d_col)


def _post_kernel(x_ref, att_ref, yt_ref, p_ref, wglu_ref, bglu_ref, gatt_ref, gssm_ref, wout_ref,
                 gmlp_ref, w1_ref, w2_ref, gple_ref, wgate_ref, wproj_ref, gfin_ref, o_ref, ys_scr):
    k = pl.program_id(1)

    @pl.when(k == 0)
    def _():
        for l in range(SSM_WIDTH // LANES):
            for i in range(CHUNK):
                ys_scr.at[l][pl.ds(i, LANES, stride=CHUNK), :] = yt_ref[i, l * LANES:(l + 1) * LANES, :].T

    rows = pl.ds(pl.multiple_of(k * POST_TILE, POST_TILE), POST_TILE)
    ys = jnp.concatenate([ys_scr[l, rows, :] for l in range(SSM_WIDTH // LANES)], axis=1)
    g = _gelu_tanh(ys)
    ssm = g * _sigmoid(_bdot(g, wglu_ref[...]) + bglu_ref[...])
    att_n = _rms(att_ref[...], gatt_ref[...])
    ssm_n = _rms(ssm, gssm_ref[...])
    h = x_ref[...] + (_bdot(att_n, wout_ref[:ATT_WIDTH, :]) + _bdot(ssm_n, wout_ref[ATT_WIDTH:, :]))
    f = _rms(h, gmlp_ref[...]).astype(BF16)
    acc = jnp.zeros_like(h)
    for kb in range(D_FF // FF_BLK):
        cols = slice(kb * FF_BLK, (kb + 1) * FF_BLK)
        t = jnp.dot(f, w1_ref[:, cols], preferred_element_type=F32)
        t = jnp.square(jnp.maximum(t, 0.0))
        acc = acc + _bdot(t, w2_ref[cols, :])
    h = h + acc
    e = _rms(h, gple_ref[...])
    h = h + _sigmoid(_bdot(e, wgate_ref[...])) * _bdot(p_ref[...], wproj_ref[...])
    o_ref[...] = _rms(h, gfin_ref[...])


def _post(x, att, yt, p, wts):
    t = x.shape[0]
    n_in = SUPER_TILE // POST_TILE
    tile = lambda width: pl.BlockSpec((POST_TILE, width), lambda i, k: (i * n_in + k, 0))

    def resident(arr):
        return pl.BlockSpec(arr.shape, lambda i, k: (0,) * arr.ndim, pipeline_mode=pl.Buffered(1))

    return pl.pallas_call(
        _post_kernel,
        grid=(t // SUPER_TILE, n_in),
        in_specs=[tile(D_MODEL), tile(ATT_WIDTH),
                  pl.BlockSpec((CHUNK, SSM_WIDTH, LANES), lambda i, k: (0, 0, i)), tile(PLE_DIM)]
                 + [resident(w) for w in wts],
        out_specs=tile(D_MODEL),
        out_shape=jax.ShapeDtypeStruct((t, D_MODEL), F32),
        scratch_shapes=[pltpu.VMEM((SSM_WIDTH // LANES, SUPER_TILE, LANES), F32)],
        compiler_params=pltpu.CompilerParams(
            dimension_semantics=("arbitrary", "arbitrary"), vmem_limit_bytes=VMEM_LIMIT),
        name="post_mixers",
    )(x, att, yt, p, *wts)


def kernel(x_prompt, x_sample, p_prompt, p_sample, rel_bias, g_mix, w_in, ssm_a_re, ssm_a_im, ssm_log_dt, ssm_b_re, ssm_b_im, ssm_c_re, ssm_c_im, ssm_d, w_glu, b_glu, g_att_out, g_ssm_out, w_out, g_mlp, w_mlp1, w_mlp2, g_ple, w_ple_gate, w_ple_proj, g_final):
    assert g_mix.shape[0] == 1, "single-layer trunk"
    seq_len = x_prompt.shape[1]
    assert x_sample.shape[1] == seq_len and seq_len % ATT_TILE == 0
    n_p, n_s = x_prompt.shape[0], x_sample.shape[0]
    assert n_p + n_s <= SEQ_PAD
    n_chunks = seq_len // CHUNK

    w_in_bf = w_in[0].astype(BF16)
    tab = _bias_tables(rel_bias)
    ssm_wts = _ssm_weights(
        ssm_a_re[0], ssm_a_im[0], ssm_log_dt[0], ssm_b_re[0], ssm_b_im[0],
        ssm_c_re[0], ssm_c_im[0], ssm_d[0])
    row = lambda v, n: v.reshape(1, n)
    wts = (w_glu[0].astype(BF16), row(b_glu[0], SSM_WIDTH), row(g_att_out[0], ATT_WIDTH),
           row(g_ssm_out[0], SSM_WIDTH), w_out[0].astype(BF16), row(g_mlp[0], D_MODEL),
           w_mlp1[0].astype(BF16), w_mlp2[0].astype(BF16), row(g_ple[0], D_MODEL),
           w_ple_gate[0].astype(BF16), w_ple_proj[0].astype(BF16), row(g_final, D_MODEL))

    groups = []
    for x3, p4 in ((x_prompt, p_prompt), (x_sample, p_sample)):
        n = x3.shape[0]
        x = x3.reshape(n * seq_len, D_MODEL)
        qkv, ut = _inproj(x, g_mix[0], w_in_bf)
        att = _attention(qkv, tab, n, seq_len)
        groups.append((x, p4[0].reshape(n * seq_len, PLE_DIM), att, ut))
    yts = _ssm(groups[0][3], groups[1][3], *ssm_wts, n_chunks)
    outs = [_post(x, att, yt, p, wts).reshape(-1, seq_len, D_MODEL)
            for (x, p, att, _), yt in zip(groups, yts)]
    return outs[0], outs[1]
```

```python
import functools
import math

import jax
import jax.numpy as jnp
import numpy as np
from jax import lax
from jax.experimental import pallas as pl
from jax.experimental.pallas import tpu as pltpu

F32 = jnp.float32
BF16 = jnp.bfloat16

D_MODEL = 1024
ATT_HEADS = 8
HEAD_DIM = 64
ATT_WIDTH = ATT_HEADS * HEAD_DIM
SSM_WIDTH = D_MODEL - ATT_WIDTH
SSM_GROUP_CH = 16
SSM_GROUPS = SSM_WIDTH // SSM_GROUP_CH
SSM_STATE = 64
IN_COLS = 3 * ATT_WIDTH + SSM_WIDTH
D_FF = 4 * D_MODEL
PLE_DIM = 256
NUM_BUCKETS = 32
REL_MAX_DISTANCE = 1024
DIL_WINDOWS = (128, 512, 2048)
DIL_RATES = (1, 4, 16)
RMS_EPS = 1e-6
NEG_INF = -1e30
LOG2_E = math.log2(math.e)

LANES = 128
SUBLANES = 8
VMEM_LIMIT = 56 * 1024 * 1024

RADIUS = 64
Q_BLK = 128
K_BLK = Q_BLK + 2 * RADIUS
ATT_TILE = 2048
PHASES = 4
HEADS_PER_STEP = LANES // HEAD_DIM
ATT_GROUP = 16
N_VARIANTS = 3

CHUNK = 16
CHUNK_COLS = CHUNK * SSM_GROUP_CH
SEQ_PAD = SUBLANES
SSM_PAIR = LANES // SSM_STATE
SCAN_UNROLL = 8

SUPER_TILE = CHUNK * LANES
IN_TILE = 1024
POST_TILE = 512
FF_BLK = 1024


def _rms(x, g):
    return x * lax.rsqrt(jnp.mean(x * x, axis=-1, keepdims=True) + RMS_EPS) * g


def _sigmoid(x):
    return 1.0 / (1.0 + jnp.exp(-x))


def _gelu_tanh(x):
    c = math.sqrt(2.0 / math.pi)
    return 0.5 * x * (1.0 + jnp.tanh(c * (x + 0.044715 * (x * x * x))))


def _bdot(a, b):
    return jnp.dot(a.astype(BF16), b.astype(BF16), preferred_element_type=F32)


def _inproj_kernel(x_ref, g_ref, w_ref, qkv_ref, ut_ref, u_scr):
    k = pl.program_id(1)
    a = _rms(x_ref[...], g_ref[...])
    z = _bdot(a, w_ref[...])
    qkv_ref[...] = z[:, :3 * ATT_WIDTH]
    rows = pl.ds(pl.multiple_of(k * IN_TILE, IN_TILE), IN_TILE)
    for l in range(SSM_WIDTH // LANES):
        u_scr[l, rows, :] = z[:, 3 * ATT_WIDTH + l * LANES:3 * ATT_WIDTH + (l + 1) * LANES]

    @pl.when(k == SUPER_TILE // IN_TILE - 1)
    def _():
        for l in range(SSM_WIDTH // LANES):
            for j in range(CHUNK):
                ut_ref[j, l * LANES:(l + 1) * LANES, :] = u_scr.at[l][pl.ds(j, LANES, stride=CHUNK), :].T


def _inproj(x, g_mix, w_in_bf):
    t = x.shape[0]
    n_in = SUPER_TILE // IN_TILE
    return pl.pallas_call(
        _inproj_kernel,
        grid=(t // SUPER_TILE, n_in),
        in_specs=[
            pl.BlockSpec((IN_TILE, D_MODEL), lambda i, k: (i * n_in + k, 0)),
            pl.BlockSpec((1, D_MODEL), lambda i, k: (0, 0)),
            pl.BlockSpec((D_MODEL, IN_COLS), lambda i, k: (0, 0)),
        ],
        out_specs=[
            pl.BlockSpec((IN_TILE, 3 * ATT_WIDTH), lambda i, k: (i * n_in + k, 0)),
            pl.BlockSpec((CHUNK, SSM_WIDTH, LANES), lambda i, k: (0, 0, i)),
        ],
        out_shape=[
            jax.ShapeDtypeStruct((t, 3 * ATT_WIDTH), F32),
            jax.ShapeDtypeStruct((CHUNK, SSM_WIDTH, t // CHUNK), F32),
        ],
        scratch_shapes=[pltpu.VMEM((SSM_WIDTH // LANES, SUPER_TILE, LANES), F32)],
        compiler_params=pltpu.CompilerParams(
            dimension_semantics=("arbitrary", "arbitrary"), vmem_limit_bytes=VMEM_LIMIT),
        name="inproj",
    )(x, g_mix.reshape(1, D_MODEL), w_in_bf)


def _t5_bucket_np(rel):
    half = NUM_BUCKETS // 2
    n = -rel
    ret = np.where(n < 0, half, 0)
    n = np.abs(n)
    max_exact = half // 2
    nf = np.maximum(n, 1).astype(np.float64)
    large = max_exact + (np.log(nf / max_exact) / math.log(REL_MAX_DISTANCE / max_exact)
                         * (half - max_exact)).astype(np.int64)
    large = np.minimum(large, half - 1)
    return ret + np.where(n < max_exact, n, large)


def _bucket_tables():
    out = np.zeros((len(DIL_RATES), N_VARIANTS, Q_BLK, K_BLK), np.int32)
    for b, d in enumerate(DIL_RATES):
        qi = np.arange(Q_BLK)
        ci = np.arange(K_BLK)
        if d == 1:
            qi = (qi % (Q_BLK // PHASES)) * PHASES + qi // (Q_BLK // PHASES)
            ci = (ci % (K_BLK // PHASES)) * PHASES + ci // (K_BLK // PHASES)
        for v, shift in enumerate((0, -RADIUS, -2 * RADIUS)):
            off = ci[None, :] - qi[:, None] + shift
            bk = _t5_bucket_np(off * d)
            out[b, v] = np.where(np.abs(off) <= RADIUS, bk, NUM_BUCKETS)
    return out.reshape(len(DIL_RATES) * N_VARIANTS, Q_BLK, K_BLK)


def _bias_kernel(rel_ref, bk_ref, tab_ref, *, buckets):
    hp = pl.program_id(0)
    for bv in range(buckets.shape[0]):
        for r0 in range(0, Q_BLK, SUBLANES):
            for c0 in range(0, K_BLK, LANES):
                rows, cols = slice(r0, r0 + SUBLANES), slice(c0, c0 + LANES)
                bk = bk_ref[bv, rows, cols]
                present = [int(b) for b in np.unique(buckets[bv, rows, cols]) if b < NUM_BUCKETS]
                hits = [bk == b for b in present]
                for h2 in range(HEADS_PER_STEP):
                    acc = jnp.full(bk.shape, NEG_INF, F32)
                    for b, hit in zip(present, hits):
                        acc = jnp.where(hit, rel_ref[b, hp * HEADS_PER_STEP + h2] * LOG2_E, acc)
                    tab_ref[0, bv * HEADS_PER_STEP + h2, rows, cols] = acc


def _bias_tables(rel_bias):
    buckets = _bucket_tables()
    bk = jnp.asarray(buckets)
    n_bv = bk.shape[0]
    n_hp = ATT_HEADS // HEADS_PER_STEP
    return pl.pallas_call(
        functools.partial(_bias_kernel, buckets=buckets),
        grid=(n_hp,),
        in_specs=[
            pl.BlockSpec(memory_space=pltpu.SMEM),
            pl.BlockSpec((n_bv, Q_BLK, K_BLK), lambda h: (0, 0, 0)),
        ],
        out_specs=pl.BlockSpec((1, n_bv * HEADS_PER_STEP, Q_BLK, K_BLK), lambda h: (h, 0, 0, 0)),
        out_shape=jax.ShapeDtypeStruct((n_hp, n_bv * HEADS_PER_STEP, Q_BLK, K_BLK), F32),
        compiler_params=pltpu.CompilerParams(dimension_semantics=("arbitrary",)),
        name="bias_tables",
    )(rel_bias, bk)


def _att_kernel(q_ref, k_ref, v_ref, tab_ref, o_ref, k4_scr, va_scr, vb_scr, qa_scr, qb_scr,
                acc_scr, m_scr, den_scr, s_scr, p_scr, *, seq_len):
    t = pl.program_id(2)
    n4 = seq_len // PHASES
    t4 = ATT_TILE // PHASES
    n_sub = ATT_TILE // Q_BLK
    stage = 256

    def head0_mask(rows):
        return lax.broadcasted_iota(jnp.int32, (rows, LANES), 1) < HEAD_DIM

    head0 = head0_mask(Q_BLK)
    head0_s = head0_mask(stage)

    @pl.when(t == 0)
    def _():
        def body(c, carry):
            for r in range(PHASES):
                src = pl.ds(r + PHASES * stage * c, stage, stride=PHASES)
                dst = pl.ds(pl.multiple_of(stage * c, stage), stage)
                k4_scr[r, dst, :] = k_ref[src, :]
                v = v_ref[src, :]
                va_scr[r, dst, :] = jnp.where(head0_s, v, 1.0)
                vb_scr[r, dst, :] = jnp.where(head0_s, 1.0, v)
            return carry
        lax.fori_loop(0, n4 // stage, body, 0)

    for r in range(PHASES):
        for c in range(t4 // stage):
            q = q_ref[pl.ds(r + PHASES * stage * c, stage, stride=PHASES), :] * (LOG2_E * HEAD_DIM ** -0.5)
            qa_scr[r, c * stage:(c + 1) * stage, :] = jnp.where(head0_s, q, 0.0)
            qb_scr[r, c * stage:(c + 1) * stage, :] = jnp.where(head0_s, 0.0, q)

    def pieces(scr, start, n):
        return jnp.concatenate([scr[r, pl.ds(start, n), :] for r in range(PHASES)], axis=0)

    def sub_block(b, idx):
        d = DIL_RATES[b]
        n_m = seq_len // d
        if d == 1:
            m0 = t * ATT_TILE + Q_BLK * idx
        elif d == PHASES:
            r, blk = idx % PHASES, idx // PHASES
            m0 = t * t4 + Q_BLK * blk
        else:
            r, r16 = idx % PHASES, idx // PHASES
            m0 = t * (ATT_TILE // d)
        k_start = jnp.clip(m0 - RADIUS, 0, n_m - K_BLK)
        variant = jnp.where(m0 < RADIUS, 0, jnp.where(m0 > n_m - Q_BLK - RADIUS, 2, 1))
        if d == 1:
            q0 = (Q_BLK // PHASES) * idx
            k0 = pl.multiple_of(k_start // PHASES, SUBLANES)
            load_q = lambda scr: pieces(scr, q0, Q_BLK // PHASES)
            load_k = lambda scr: pieces(scr, k0, K_BLK // PHASES)

            def store(scr, val):
                n = Q_BLK // PHASES
                for rr in range(PHASES):
                    scr[b, rr, q0:q0 + n, :] = val[rr * n:(rr + 1) * n]
        elif d == PHASES:
            load_q = lambda scr: scr[r, blk * Q_BLK:(blk + 1) * Q_BLK, :]
            load_k = lambda scr: scr[r, pl.ds(pl.multiple_of(k_start, SUBLANES), K_BLK), :]

            def store(scr, val):
                scr[b, r, blk * Q_BLK:(blk + 1) * Q_BLK, :] = val
        else:
            e = d // PHASES
            load_q = lambda scr: scr.at[r][pl.ds(r16, Q_BLK, stride=e), :]
            load_k = lambda scr: scr.at[r][pl.ds(r16 + e * k_start, K_BLK, stride=e), :]

            def store(scr, val):
                scr.at[b, r][pl.ds(r16, Q_BLK, stride=e), :] = val
        return load_q, load_k, store, variant

    for b in range(len(DIL_RATES)):
        for g0 in range(0, n_sub, ATT_GROUP):
            stores, key_loaders = [], []
            for j in range(ATT_GROUP):
                load_q, load_k, store, variant = sub_block(b, g0 + j)
                stores.append(store)
                key_loaders.append(load_k)
                kb = load_k(k4_scr).astype(BF16)
                for h2, q_scr in enumerate((qa_scr, qb_scr)):
                    s = lax.dot_general(load_q(q_scr).astype(BF16), kb, (((1,), (1,)), ((), ())),
                                        preferred_element_type=F32)
                    s_scr[j * HEADS_PER_STEP + h2] = (
                        s + tab_ref[0, (b * N_VARIANTS + variant) * HEADS_PER_STEP + h2])
            for j in range(ATT_GROUP):
                ms = []
                for h2 in range(HEADS_PER_STEP):
                    s = s_scr[j * HEADS_PER_STEP + h2]
                    m = jnp.max(s, axis=-1, keepdims=True)
                    p_scr[j * HEADS_PER_STEP + h2] = jnp.exp2((s - m).astype(BF16))
                    ms.append(m)
                stores[j](m_scr, jnp.where(head0, ms[0], ms[1]))
            for j in range(ATT_GROUP):
                outs = [jnp.dot(p_scr[j * HEADS_PER_STEP + h2], key_loaders[j](vh_scr).astype(BF16),
                                preferred_element_type=F32)
                        for h2, vh_scr in enumerate((va_scr, vb_scr))]
                stores[j](acc_scr, jnp.where(head0, outs[0], outs[1]))
                stores[j](den_scr, jnp.where(head0, outs[1], outs[0]))

    for r in range(PHASES):
        m_all = m_scr[:, r]
        m = jnp.max(m_all, axis=0)
        num = jnp.zeros((t4, LANES), F32)
        den = jnp.zeros((t4, LANES), F32)
        for b in range(len(DIL_RATES)):
            w = jnp.exp2(m_all[b] - m)
            num = num + w * acc_scr[b, r]
            den = den + w * pltpu.roll(den_scr[b, r], HEAD_DIM, axis=1)
        o_ref[pl.ds(r, t4, stride=PHASES), :] = num / den


def _attention(qkv, tab, n_seq, seq_len):
    n_hp = ATT_HEADS // HEADS_PER_STEP
    n_t = seq_len // ATT_TILE
    n_br = len(DIL_RATES)
    return pl.pallas_call(
        functools.partial(_att_kernel, seq_len=seq_len),
        grid=(n_seq, n_hp, n_t),
        in_specs=[
            pl.BlockSpec((ATT_TILE, LANES), lambda s, h, t: (s * n_t + t, h)),
            pl.BlockSpec((seq_len, LANES), lambda s, h, t: (s, n_hp + h)),
            pl.BlockSpec((seq_len, LANES), lambda s, h, t: (s, 2 * n_hp + h)),
            pl.BlockSpec((1,) + tab.shape[1:], lambda s, h, t: (h, 0, 0, 0)),
        ],
        out_specs=pl.BlockSpec((ATT_TILE, LANES), lambda s, h, t: (s * n_t + t, h)),
        out_shape=jax.ShapeDtypeStruct((n_seq * seq_len, ATT_WIDTH), F32),
        scratch_shapes=[
            pltpu.VMEM((PHASES, seq_len // PHASES, LANES), F32),
            pltpu.VMEM((PHASES, seq_len // PHASES, LANES), F32),
            pltpu.VMEM((PHASES, seq_len // PHASES, LANES), F32),
            pltpu.VMEM((PHASES, ATT_TILE // PHASES, LANES), F32),
            pltpu.VMEM((PHASES, ATT_TILE // PHASES, LANES), F32),
            pltpu.VMEM((n_br, PHASES, ATT_TILE // PHASES, LANES), F32),
            pltpu.VMEM((n_br, PHASES, ATT_TILE // PHASES, LANES), F32),
            pltpu.VMEM((n_br, PHASES, ATT_TILE // PHASES, LANES), F32),
            pltpu.VMEM((ATT_GROUP * HEADS_PER_STEP, Q_BLK, K_BLK), F32),
            pltpu.VMEM((ATT_GROUP * HEADS_PER_STEP, Q_BLK, K_BLK), BF16),
        ],
        compiler_params=pltpu.CompilerParams(
            dimension_semantics=("arbitrary", "arbitrary", "arbitrary"),
            vmem_limit_bytes=VMEM_LIMIT),
        name="dilated_attention",
    )(qkv, qkv, qkv, tab)


def _ssm_weights(a_re, a_im, log_dt, b_re, b_im, c_re, c_im, d_skip):
    hi = lax.Precision.HIGHEST
    G, N, HC = SSM_GROUPS, SSM_STATE, SSM_GROUP_CH
    dt = jnp.exp(log_dt)[..., None]
    mag = jnp.exp(a_re * dt)
    ab_re = mag * jnp.cos(a_im * dt)
    ab_im = mag * jnp.sin(a_im * dt)
    inv = 1.0 / (a_re * a_re + a_im * a_im)
    f_re = ((ab_re - 1.0) * a_re + ab_im * a_im) * inv
    f_im = (ab_im * a_re - (ab_re - 1.0) * a_im) * inv
    bb_re = f_re[..., None] * b_re - f_im[..., None] * b_im
    bb_im = f_re[..., None] * b_im + f_im[..., None] * b_re
    tau = jnp.arange(CHUNK + 1, dtype=F32)
    mag_t = jnp.exp((a_re * dt)[..., None] * tau)
    pr = mag_t * jnp.cos((a_im * dt)[..., None] * tau)
    pi = mag_t * jnp.sin((a_im * dt)[..., None] * tau)

    def a_pow_b(direction, rev):
        order = slice(CHUNK - 1, None, -1) if rev else slice(0, CHUNK)
        pr_c = jnp.repeat(pr[direction][..., order], HC, axis=-1)
        pi_c = jnp.repeat(pi[direction][..., order], HC, axis=-1)
        br = jnp.tile(bb_re[direction], (1, 1, CHUNK))
        bi = jnp.tile(bb_im[direction], (1, 1, CHUNK))
        return pr_c * br - pi_c * bi, pr_c * bi + pi_c * br

    def lag_table(direction, ab):
        return (jnp.einsum('gcn,gnx->gcx', c_re[direction], ab[0], precision=hi)
                - jnp.einsum('gcn,gnx->gcx', c_im[direction], ab[1], precision=hi))

    ab_f = a_pow_b(0, True)
    ab_b = a_pow_b(1, False)
    lag = jnp.stack([lag_table(0, ab_f), lag_table(1, ab_b)], axis=1)
    w_state = jnp.concatenate([ab_f[0], ab_f[1], ab_b[0], ab_b[1]], axis=1).astype(BF16)
    pr = jnp.moveaxis(pr, -1, 0)
    pi = jnp.moveaxis(pi, -1, 0)

    def state_out(direction, pr_sel, pi_sel):
        cr = c_re[direction][:, None, :, :]
        ci = c_im[direction][:, None, :, :]
        pr_s = pr_sel.transpose(1, 0, 2)[:, :, None, :]
        pi_s = pi_sel.transpose(1, 0, 2)[:, :, None, :]
        from_re = (cr * pr_s - ci * pi_s).reshape(G, CHUNK_COLS, N)
        from_im = (-cr * pi_s - ci * pr_s).reshape(G, CHUNK_COLS, N)
        return jnp.concatenate([from_re, from_im], axis=2)

    c_pow = jnp.concatenate(
        [state_out(0, pr[1:CHUNK + 1, 0], pi[1:CHUNK + 1, 0]),
         state_out(1, pr[CHUNK:0:-1, 1], pi[CHUNK:0:-1, 1])], axis=2)
    pair_eye = jnp.eye(SSM_PAIR, dtype=F32)[None, :, None, None, :, None]
    c_pow = (c_pow.reshape(G // SSM_PAIR, SSM_PAIR, CHUNK_COLS, 4, 1, N) * pair_eye).reshape(
        G // SSM_PAIR, SSM_PAIR * CHUNK_COLS, 4 * SSM_PAIR * N).astype(BF16)

    a_chunk = jnp.stack([pr[CHUNK, 0], pi[CHUNK, 0], pr[CHUNK, 1], pi[CHUNK, 1]], axis=1)
    a_chunk = a_chunk.reshape(G // SSM_PAIR, SSM_PAIR, 4, N).transpose(0, 2, 1, 3).reshape(
        G // SSM_PAIR, 4, SSM_PAIR * N)
    d_col = jnp.tile(d_skip[:, None, :], (1, CHUNK, 1)).reshape(G, CHUNK_COLS, 1)
    return lag, w_state, c_pow, a_chunk, d_col


def _ssm_kernel(utp_ref, uts_ref, lag_ref, ws_ref, cp_ref, a_ref, d_ref, ytp_ref, yts_ref, g_scr, h_scr,
                *, n_chunks, n_p, n_s):
    seqs = [(utp_ref, ytp_ref, s) for s in range(n_p)] + [(uts_ref, yts_ref, s) for s in range(n_s)]
    n_st = SSM_STATE
    pair = range(SSM_PAIR)

    def group_rows(q):
        return slice(q * SSM_GROUP_CH, (q + 1) * SSM_GROUP_CH)

    @pl.when(pl.program_id(0) == 0)
    def _():
        g_scr[...] = jnp.zeros(g_scr.shape, F32)

    col_j = lax.broadcasted_iota(jnp.int32, (SSM_GROUP_CH, CHUNK_COLS), 1) // SSM_GROUP_CH
    w_intra = []
    for q in pair:
        lag_f, lag_b = lag_ref[q, 0], lag_ref[q, 1]
        blocks = []
        for i in range(CHUNK):
            fwd = pltpu.roll(lag_f, (i + 1) * SSM_GROUP_CH % CHUNK_COLS, axis=1)
            bwd = pltpu.roll(lag_b, i * SSM_GROUP_CH, axis=1)
            blocks.append(jnp.where(col_j <= i, fwd, 0.0) + jnp.where(col_j >= i, bwd, 0.0))
        w_intra.append(jnp.concatenate(blocks, axis=0).astype(BF16))

    for slot, (u_ref, y_ref, s) in enumerate(seqs):
        lanes = slice(s * n_chunks, (s + 1) * n_chunks)
        state_in = []
        for q in pair:
            x_t = u_ref[:, group_rows(q), lanes].reshape(CHUNK_COLS, n_chunks).astype(BF16)
            r = jnp.dot(w_intra[q], x_t, preferred_element_type=F32)
            y_ref[:, group_rows(q), lanes] = r.reshape(CHUNK, SSM_GROUP_CH, n_chunks)
            state_in.append(jnp.dot(ws_ref[q], x_t, preferred_element_type=F32))
        for k in range(4):
            both = jnp.concatenate([state_in[q][k * n_st:(k + 1) * n_st] for q in pair], axis=0)
            g_scr.at[k][pl.ds(slot, n_chunks, stride=SEQ_PAD), :] = both.T

    a = a_ref[0]
    shape = (SEQ_PAD, LANES)
    prf, pif, prb, pib = [jnp.broadcast_to(a[k:k + 1], shape) for k in range(4)]

    def step(c, carry):
        fr, fi, br, bi = carry
        rf = pl.ds(pl.multiple_of(c * SEQ_PAD, SEQ_PAD), SEQ_PAD)
        rb = pl.ds(pl.multiple_of((n_chunks - 1 - c) * SEQ_PAD, SEQ_PAD), SEQ_PAD)
        h_scr[0, rf, :] = fr
        h_scr[1, rf, :] = fi
        h_scr[2, rb, :] = br
        h_scr[3, rb, :] = bi
        return (prf * fr - pif * fi + g_scr[0, rf, :], prf * fi + pif * fr + g_scr[1, rf, :],
                prb * br - pib * bi + g_scr[2, rb, :], prb * bi + pib * br + g_scr[3, rb, :])

    zero = jnp.zeros(shape, F32)
    lax.fori_loop(0, n_chunks, step, (zero, zero, zero, zero), unroll=SCAN_UNROLL)

    for slot, (u_ref, y_ref, s) in enumerate(seqs):
        lanes = slice(s * n_chunks, (s + 1) * n_chunks)
        h_rows = jnp.concatenate(
            [h_scr.at[k][pl.ds(slot, n_chunks, stride=SEQ_PAD), :] for k in range(4)], axis=1)
        y_state = lax.dot_general(cp_ref[0], h_rows.astype(BF16), (((1,), (1,)), ((), ())),
                                  preferred_element_type=F32)
        for q in pair:
            x_t = u_ref[:, group_rows(q), lanes].reshape(CHUNK_COLS, n_chunks)
            y = (y_ref[:, group_rows(q), lanes].reshape(CHUNK_COLS, n_chunks)
                 + y_state[q * CHUNK_COLS:(q + 1) * CHUNK_COLS] + d_ref[q] * x_t)
            y_ref[:, group_rows(q), lanes] = y.reshape(CHUNK, SSM_GROUP_CH, n_chunks)


def _ssm(ut_p, ut_s, lag, w_state, c_pow, a_chunk, d_col, n_chunks):
    G = SSM_GROUPS // SSM_PAIR
    n_p = ut_p.shape[2] // n_chunks
    n_s = ut_s.shape[2] // n_chunks
    n_rows = n_chunks * SEQ_PAD
    act = lambda arr: pl.BlockSpec((CHUNK, SSM_PAIR * SSM_GROUP_CH, arr.shape[2]), lambda g: (0, g, 0))
    per_group = lambda arr: pl.BlockSpec(
        (arr.shape[0] // G,) + arr.shape[1:], lambda g: (g,) + (0,) * (arr.ndim - 1))
    return pl.pallas_call(
        functools.partial(_ssm_kernel, n_chunks=n_chunks, n_p=n_p, n_s=n_s),
        grid=(G,),
        in_specs=[act(ut_p), act(ut_s), per_group(lag), per_group(w_state), per_group(c_pow),
                  per_group(a_chunk), per_group(d_col)],
        out_specs=[act(ut_p), act(ut_s)],
        out_shape=[jax.ShapeDtypeStruct(ut_p.shape, F32), jax.ShapeDtypeStruct(ut_s.shape, F32)],
        scratch_shapes=[
            pltpu.VMEM((4, n_rows, LANES), F32),
            pltpu.VMEM((4, n_rows, LANES), F32),
        ],
        compiler_params=pltpu.CompilerParams(
            dimension_semantics=("arbitrary",), vmem_limit_bytes=VMEM_LIMIT),
        name="ssm_chunked",
    )(ut_p, ut_s, lag, w_state, c_pow, a_chunk, d_col)


def _post_kernel(x_ref, att_ref, yt_ref, p_ref, wglu_ref, bglu_ref, gatt_ref, gssm_ref, wout_ref,
                 gmlp_ref, w1_ref, w2_ref, gple_ref, wgate_ref, wproj_ref, gfin_ref, o_ref, ys_scr):
    k = pl.program_id(1)

    @pl.when(k == 0)
    def _():
        for l in range(SSM_WIDTH // LANES):
            for i in range(CHUNK):
                ys_scr.at[l][pl.ds(i, LANES, stride=CHUNK), :] = yt_ref[i, l * LANES:(l + 1) * LANES, :].T

    rows = pl.ds(pl.multiple_of(k * POST_TILE, POST_TILE), POST_TILE)
    ys = jnp.concatenate([ys_scr[l, rows, :] for l in range(SSM_WIDTH // LANES)], axis=1)
    g = _gelu_tanh(ys)
    ssm = g * _sigmoid(_bdot(g, wglu_ref[...]) + bglu_ref[...])
    att_n = _rms(att_ref[...], gatt_ref[...])
    ssm_n = _rms(ssm, gssm_ref[...])
    h = x_ref[...] + (_bdot(att_n, wout_ref[:ATT_WIDTH, :]) + _bdot(ssm_n, wout_ref[ATT_WIDTH:, :]))
    f = _rms(h, gmlp_ref[...]).astype(BF16)
    acc = jnp.zeros_like(h)
    for kb in range(D_FF // FF_BLK):
        cols = slice(kb * FF_BLK, (kb + 1) * FF_BLK)
        t = jnp.dot(f, w1_ref[:, cols], preferred_element_type=F32)
        t = jnp.square(jnp.maximum(t, 0.0))
        acc = acc + _bdot(t, w2_ref[cols, :])
    h = h + acc
    e = _rms(h, gple_ref[...])
    h = h + _sigmoid(_bdot(e, wgate_ref[...])) * _bdot(p_ref[...], wproj_ref[...])
    o_ref[...] = _rms(h, gfin_ref[...])


def _post(x, att, yt, p, wts):
    t = x.shape[0]
    n_in = SUPER_TILE // POST_TILE
    tile = lambda width: pl.BlockSpec((POST_TILE, width), lambda i, k: (i * n_in + k, 0))

    def resident(arr):
        return pl.BlockSpec(arr.shape, lambda i, k: (0,) * arr.ndim, pipeline_mode=pl.Buffered(1))

    return pl.pallas_call(
        _post_kernel,
        grid=(t // SUPER_TILE, n_in),
        in_specs=[tile(D_MODEL), tile(ATT_WIDTH),
                  pl.BlockSpec((CHUNK, SSM_WIDTH, LANES), lambda i, k: (0, 0, i)), tile(PLE_DIM)]
                 + [resident(w) for w in wts],
        out_specs=tile(D_MODEL),
        out_shape=jax.ShapeDtypeStruct((t, D_MODEL), F32),
        scratch_shapes=[pltpu.VMEM((SSM_WIDTH // LANES, SUPER_TILE, LANES), F32)],
        compiler_params=pltpu.CompilerParams(
            dimension_semantics=("arbitrary", "arbitrary"), vmem_limit_bytes=VMEM_LIMIT),
        name="post_mixers",
    )(x, att, yt, p, *wts)


def kernel(x_prompt, x_sample, p_prompt, p_sample, rel_bias, g_mix, w_in, ssm_a_re, ssm_a_im, ssm_log_dt, ssm_b_re, ssm_b_im, ssm_c_re, ssm_c_im, ssm_d, w_glu, b_glu, g_att_out, g_ssm_out, w_out, g_mlp, w_mlp1, w_mlp2, g_ple, w_ple_gate, w_ple_proj, g_final):
    assert g_mix.shape[0] == 1, "single-layer trunk"
    seq_len = x_prompt.shape[1]
    assert x_sample.shape[1] == seq_len and seq_len % ATT_TILE == 0
    n_p, n_s = x_prompt.shape[0], x_sample.shape[0]
    assert n_p + n_s <= SEQ_PAD
    n_chunks = seq_len // CHUNK

    w_in_bf = w_in[0].astype(BF16)
    tab = _bias_tables(rel_bias)
    ssm_wts = _ssm_weights(
        ssm_a_re[0], ssm_a_im[0], ssm_log_dt[0], ssm_b_re[0], ssm_b_im[0],
        ssm_c_re[0], ssm_c_im[0], ssm_d[0])
    row = lambda v, n: v.reshape(1, n)
    wts = (w_glu[0].astype(BF16), row(b_glu[0], SSM_WIDTH), row(g_att_out[0], ATT_WIDTH),
           row(g_ssm_out[0], SSM_WIDTH), w_out[0].astype(BF16), row(g_mlp[0], D_MODEL),
           w_mlp1[0].astype(BF16), w_mlp2[0].astype(BF16), row(g_ple[0], D_MODEL),
           w_ple_gate[0].astype(BF16), w_ple_proj[0].astype(BF16), row(g_final, D_MODEL))

    groups = []
    for x3, p4 in ((x_prompt, p_prompt), (x_sample, p_sample)):
        n = x3.shape[0]
        x = x3.reshape(n * seq_len, D_MODEL)
        qkv, ut = _inproj(x, g_mix[0], w_in_bf)
        att = _attention(qkv, tab, n, seq_len)
        groups.append((x, p4[0].reshape(n * seq_len, PLE_DIM), att, ut))
    yts = _ssm(groups[0][3], groups[1][3], *ssm_wts, n_chunks)
    outs = [_post(x, att, yt, p, wts).reshape(-1, seq_len, D_MODEL)
            for (x, p, att, _), yt in zip(groups, yts)]
    return outs[0], outs[1]
```
